```python
import jax, jax.numpy as jnp
from jax import lax
import numpy as np

D_MODEL = 1024
BATCH = 8
SEQ = 2048
DEPTH = 2

BRANCH_W = D_MODEL // 2
LRU_W = BRANCH_W
LRU_BLOCKS = 8
LRU_BLOCK_W = LRU_W // LRU_BLOCKS
LRU_CONV = 4
LRU_C = 8.0
SC_W = BRANCH_W
SC_CONV = 3
HEAD_DIM = 64
N_Q_HEADS = BRANCH_W // HEAD_DIM
N_KV_HEADS = 2
GQA_GROUP = N_Q_HEADS // N_KV_HEADS
WINDOW = 128
BLOCK = 128
CF_W = BRANCH_W
CF_CONV = 31
N_BRANCHES = 4
D_FF = -(-8 * D_MODEL // (3 * 256)) * 256
EPS = 1e-6
NEG_INF = -1e30

SPLIT_SIZES = (
    LRU_W, LRU_W,
    SC_W, SC_W, SC_W,
    N_Q_HEADS * HEAD_DIM,
    N_KV_HEADS * HEAD_DIM,
    N_KV_HEADS * HEAD_DIM,
    2 * CF_W,
    N_BRANCHES * D_MODEL,
)
IN_W = int(sum(SPLIT_SIZES))
SPLIT_POINTS = tuple(int(v) for v in np.cumsum(SPLIT_SIZES)[:-1])

kernel_name = "hybrid_rglru_shortconv_swa_conformer"


def rms_norm(x, g):
    xf = x.astype(jnp.float32)
    y = xf * lax.rsqrt(jnp.mean(xf * xf, axis=-1, keepdims=True) + EPS)
    return (y * g.astype(jnp.float32)).astype(x.dtype)


def layer_norm(x, g, b):
    xf = x.astype(jnp.float32)
    mu = jnp.mean(xf, axis=-1, keepdims=True)
    var = jnp.mean(jnp.square(xf - mu), axis=-1, keepdims=True)
    y = (xf - mu) * lax.rsqrt(var + EPS)
    return (y * g.astype(jnp.float32) + b.astype(jnp.float32)).astype(x.dtype)


def causal_dwconv(x, w, b=None):
    k, c = w.shape
    y = lax.conv_general_dilated(
        x, w[:, None, :].astype(x.dtype), window_strides=(1,), padding=[(k - 1, 0)],
        dimension_numbers=("NWC", "WIO", "NWC"), feature_group_count=c)
    if b is not None:
        y = y + b
    return y


def rg_lru(x, wx, bx, wa, ba, lam):
    b, s, w = x.shape
    xb = x.reshape(b, s, LRU_BLOCKS, LRU_BLOCK_W)
    gate_i = jax.nn.sigmoid(jnp.einsum("bshi,hij->bshj", xb, wx).reshape(b, s, w) + bx)
    gate_r = jax.nn.sigmoid(jnp.einsum("bshi,hij->bshj", xb, wa).reshape(b, s, w) + ba)
    log_a = -LRU_C * gate_r.astype(jnp.float32) * jax.nn.softplus(-lam.astype(jnp.float32))
    a = jnp.exp(log_a)
    mult = jnp.sqrt(-jnp.expm1(2.0 * log_a))
    u = (x * gate_i).astype(jnp.float32) * mult

    def combine(left, right):
        a_l, b_l = left
        a_r, b_r = right
        return a_l * a_r, a_r * b_l + b_r

    _, h = lax.associative_scan(combine, (a, u), axis=1)
    return h.astype(x.dtype)


def alibi_slopes(n):
    return jnp.asarray([2.0 ** (-8.0 * (i + 1) / n) for i in range(n)], dtype=jnp.float32)


def sliding_window_attention(q, k, v, sinks):
    b, s, _ = q.shape
    nb = s // BLOCK
    q = q.reshape(b, nb, BLOCK, N_KV_HEADS, GQA_GROUP, HEAD_DIM)
    k = k.reshape(b, nb, BLOCK, N_KV_HEADS, HEAD_DIM)
    v = v.reshape(b, nb, BLOCK, N_KV_HEADS, HEAD_DIM)
    zero = jnp.zeros_like(k[:, :1])
    k2 = jnp.concatenate([jnp.concatenate([zero, k[:, :-1]], axis=1), k], axis=2)
    v2 = jnp.concatenate([jnp.concatenate([zero, v[:, :-1]], axis=1), v], axis=2)
    scores = jnp.einsum("bnqhgd,bnkhd->bnhgqk", q, k2).astype(jnp.float32) * (HEAD_DIM ** -0.5)
    qi = jnp.arange(BLOCK)[:, None]
    ki = jnp.arange(2 * BLOCK)[None, :]
    dist = qi + BLOCK - ki
    key_pos = (jnp.arange(nb)[:, None, None] - 1) * BLOCK + ki[None]
    valid = (dist >= 0)[None] & (dist < WINDOW)[None] & (key_pos >= 0)
    slopes = alibi_slopes(N_Q_HEADS).reshape(N_KV_HEADS, GQA_GROUP)
    scores = scores - slopes[:, :, None, None] * dist.astype(jnp.float32)
    scores = jnp.where(valid[None, :, None, None], scores, NEG_INF)
    sink = jnp.broadcast_to(
        sinks.astype(jnp.float32).reshape(1, 1, N_KV_HEADS, GQA_GROUP, 1, 1),
        scores.shape[:-1] + (1,))
    probs = jax.nn.softmax(jnp.concatenate([scores, sink], axis=-1), axis=-1)[..., :-1]
    out = jnp.einsum("bnhgqk,bnkhd->bnqhgd", probs.astype(v2.dtype), v2)
    return out.reshape(b, s, N_Q_HEADS * HEAD_DIM)


def hybrid_mixer(xn, w_in, conv_a_w, conv_a_b, lru_wx, lru_bx, lru_wa, lru_ba, lru_lambda,
                 w_a_out, conv_b_w, w_b_out, sinks, w_c_out, conv_d_w, conv_d_b,
                 ln_d_g, ln_d_b, w_d_out, w_o):
    b, s, _ = xn.shape
    proj = xn @ w_in
    a_x, a_gate, b_v, b_c, b_b, q, k, v, d_in, gate_logits = jnp.split(proj, SPLIT_POINTS, axis=-1)
    a_h = rg_lru(causal_dwconv(a_x, conv_a_w, conv_a_b), lru_wx, lru_bx, lru_wa, lru_ba, lru_lambda)
    y_a = (a_h * jax.nn.gelu(a_gate)) @ w_a_out
    y_b = (b_b * causal_dwconv(b_c * b_v, conv_b_w)) @ w_b_out
    y_c = sliding_window_attention(q, k, v, sinks) @ w_c_out
    d = d_in[..., :CF_W] * jax.nn.sigmoid(d_in[..., CF_W:])
    d = jax.nn.silu(layer_norm(causal_dwconv(d, conv_d_w, conv_d_b), ln_d_g, ln_d_b))
    y_d = d @ w_d_out
    g = jax.nn.sigmoid(gate_logits).reshape(b, s, N_BRANCHES, D_MODEL)
    merged = g[:, :, 0] * y_a + g[:, :, 1] * y_b + g[:, :, 2] * y_c + g[:, :, 3] * y_d
    return merged @ w_o


def swiglu(x, w_gate, w_up, w_down):
    return (jax.nn.silu(x @ w_gate) * (x @ w_up)) @ w_down


def setup_inputs(seed: int = 0) -> dict:
    key = jax.random.key(seed)
    ks = jax.random.split(key, 32)
    f32 = jnp.float32

    def nrm(k, shape, scale):
        return jax.random.normal(k, shape, f32) * scale

    L = DEPTH
    target = jax.random.uniform(ks[9], (L, LRU_W), f32, 0.9, 0.999)
    sig = target ** (1.0 / LRU_C)
    lru_lambda = jnp.log(sig) - jnp.log1p(-sig)
    return {
        "x": nrm(ks[0], (BATCH, SEQ, D_MODEL), 1.0),
        "norm1_g": 1.0 + nrm(ks[1], (L, D_MODEL), 0.02),
        "w_in": nrm(ks[2], (L, D_MODEL, IN_W), D_MODEL ** -0.5),
        "conv_a_w": nrm(ks[3], (L, LRU_CONV, LRU_W), LRU_CONV ** -0.5),
        "conv_a_b": nrm(ks[4], (L, LRU_W), 0.02),
        "lru_wx": nrm(ks[5], (L, LRU_BLOCKS, LRU_BLOCK_W, LRU_BLOCK_W), LRU_BLOCK_W ** -0.5),
        "lru_bx": nrm(ks[6], (L, LRU_W), 0.02),
        "lru_wa": nrm(ks[7], (L, LRU_BLOCKS, LRU_BLOCK_W, LRU_BLOCK_W), LRU_BLOCK_W ** -0.5),
        "lru_ba": nrm(ks[8], (L, LRU_W), 0.02),
        "lru_lambda": lru_lambda,
        "w_a_out": nrm(ks[10], (L, LRU_W, D_MODEL), LRU_W ** -0.5),
        "conv_b_w": nrm(ks[11], (L, SC_CONV, SC_W), SC_CONV ** -0.5),
        "w_b_out": nrm(ks[12], (L, SC_W, D_MODEL), SC_W ** -0.5),
        "sinks": nrm(ks[13], (L, N_Q_HEADS), 0.5),
        "w_c_out": nrm(ks[14], (L, N_Q_HEADS * HEAD_DIM, D_MODEL), (N_Q_HEADS * HEAD_DIM) ** -0.5),
        "conv_d_w": nrm(ks[15], (L, CF_CONV, CF_W), CF_CONV ** -0.5),
        "conv_d_b": nrm(ks[16], (L, CF_W), 0.02),
        "ln_d_g": 1.0 + nrm(ks[17], (L, CF_W), 0.02),
        "ln_d_b": nrm(ks[18], (L, CF_W), 0.02),
        "w_d_out": nrm(ks[19], (L, CF_W, D_MODEL), CF_W ** -0.5),
        "w_o": nrm(ks[20], (L, D_MODEL, D_MODEL), D_MODEL ** -0.5),
        "norm2_g": 1.0 + nrm(ks[21], (L, D_MODEL), 0.02),
        "w_ffn_gate": nrm(ks[22], (L, D_MODEL, D_FF), D_MODEL ** -0.5),
        "w_ffn_up": nrm(ks[23], (L, D_MODEL, D_FF), D_MODEL ** -0.5),
        "w_ffn_down": nrm(ks[24], (L, D_FF, D_MODEL), D_FF ** -0.5),
        "final_g": 1.0 + nrm(ks[25], (D_MODEL,), 0.02),
    }


def reference(x, norm1_g, w_in, conv_a_w, conv_a_b, lru_wx, lru_bx, lru_wa, lru_ba, lru_lambda,
              w_a_out, conv_b_w, w_b_out, sinks, w_c_out, conv_d_w, conv_d_b, ln_d_g, ln_d_b,
              w_d_out, w_o, norm2_g, w_ffn_gate, w_ffn_up, w_ffn_down, final_g):
    for l in range(DEPTH):
        xn = rms_norm(x, norm1_g[l])
        x = x + hybrid_mixer(xn, w_in[l], conv_a_w[l], conv_a_b[l], lru_wx[l], lru_bx[l],
                             lru_wa[l], lru_ba[l], lru_lambda[l], w_a_out[l], conv_b_w[l],
                             w_b_out[l], sinks[l], w_c_out[l], conv_d_w[l], conv_d_b[l],
                             ln_d_g[l], ln_d_b[l], w_d_out[l], w_o[l])
        x = x + swiglu(rms_norm(x, norm2_g[l]), w_ffn_gate[l], w_ffn_up[l], w_ffn_down[l])
    return rms_norm(x, final_g)
```

```python
import functools

import jax
import jax.numpy as jnp
import numpy as np
from jax import lax
from jax.experimental import pallas as pl
from jax.experimental.pallas import tpu as pltpu

D_MODEL = 1024
BRANCH_W = 512
LRU_BLOCKS = 8
LRU_CONV = 4
LRU_C = 8.0
SC_CONV = 3
HEAD_DIM = 64
N_Q_HEADS = 8
N_KV_HEADS = 2
GQA_GROUP = N_Q_HEADS // N_KV_HEADS
WINDOW = 128
BLOCK = 128
CF_CONV = 31
N_BRANCHES = 4
D_FF = 2816
EPS = 1e-6
NEG_INF = -1e30

OFF_A = 0
OFF_B = OFF_A + 2 * BRANCH_W
OFF_C = OFF_B + 3 * BRANCH_W
OFF_D = OFF_C + BRANCH_W + 2 * N_KV_HEADS * HEAD_DIM
OFF_G = OFF_D + 2 * BRANCH_W
IN_W = OFF_G + N_BRANCHES * D_MODEL

SUBLANES = 8
LANES = 128
CONV_A_HALO = 8
CONV_D_HALO = 32
TS = 256
TM = 512
SCAN_SEGS = SUBLANES
VMEM_LIMIT_MIXER = 56 * 1024 * 1024
VMEM_LIMIT_FFN = 48 * 1024 * 1024

ALIBI_SLOPES = tuple(2.0 ** (-8.0 * (i + 1) / N_Q_HEADS) for i in range(N_Q_HEADS))


def _rms(x, g):
    return x * lax.rsqrt(jnp.mean(x * x, axis=-1, keepdims=True) + EPS) * g


def _bdot(a, b):
    return jnp.dot(a.astype(jnp.bfloat16), b, preferred_element_type=jnp.float32)


def _causal_conv(buf_ref, w_ref, halo, width, rows):
    acc = None
    for k in range(width):
        term = w_ref[k:k + 1, :] * buf_ref[pl.ds(halo - (width - 1) + k, rows), :]
        acc = term if acc is None else acc + term
    return acc


def _mixer_kernel(sinks_ref, x_ref, n1g_ref, w_in_ref, caw_ref, cab_ref, wx_ref, bx_ref, wa_ref, ba_ref,
                  lam_ref, wao_ref, cbw_ref, wbo_ref, wco_ref, cdw_ref, cdb_ref, lng_ref, lnb_ref, wdo_ref,
                  wo_ref, o_ref, ax_buf, cv_buf, d_buf, kv_buf, hc_ref, sa_ref, su_ref, sp_ref, sh_ref):
    j = pl.program_id(1)
    ts = x_ref.shape[1]
    seg_len = ts // SCAN_SEGS

    @pl.when(j == 0)
    def _():
        ax_buf[0:CONV_A_HALO, :] = jnp.zeros((CONV_A_HALO, BRANCH_W), jnp.float32)
        cv_buf[0:CONV_A_HALO, :] = jnp.zeros((CONV_A_HALO, BRANCH_W), jnp.float32)
        d_buf[0:CONV_D_HALO, :] = jnp.zeros((CONV_D_HALO, BRANCH_W), jnp.float32)
        kv_buf[:, 0:BLOCK, :] = jnp.zeros((6, BLOCK, LANES), jnp.bfloat16)
        hc_ref[...] = jnp.zeros_like(hc_ref)

    x = x_ref[0]
    xn = _rms(x, n1g_ref[...]).astype(jnp.bfloat16)

    def proj(off, width):
        return jnp.dot(xn, w_in_ref[:, off:off + width], preferred_element_type=jnp.float32)

    def gate(k):
        return jax.nn.sigmoid(proj(OFF_G + k * D_MODEL, D_MODEL))

    p_a = proj(OFF_A, 2 * BRANCH_W)
    ax_buf[CONV_A_HALO:CONV_A_HALO + ts, :] = p_a[:, :BRANCH_W]
    c = _causal_conv(ax_buf, caw_ref, CONV_A_HALO, LRU_CONV, ts) + cab_ref[...]
    ax_buf[0:CONV_A_HALO, :] = ax_buf[ts:ts + CONV_A_HALO, :]
    gate_i = jax.nn.sigmoid(_bdot(c, wx_ref[...]) + bx_ref[...])
    gate_r = jax.nn.sigmoid(_bdot(c, wa_ref[...]) + ba_ref[...])
    z = -lam_ref[...]
    softplus = jnp.maximum(z, 0.0) + jnp.log1p(jnp.exp(-jnp.abs(z)))
    log_a = -LRU_C * gate_r * softplus
    a = jnp.exp(log_a)
    u = (c * gate_i) * jnp.sqrt(-jnp.tanh(log_a) * (a * a + 1.0))
    h_groups = []
    for g in range(BRANCH_W // LANES):
        lanes = slice(g * LANES, (g + 1) * LANES)
        sa_ref[g] = a[:, lanes]
        su_ref[g] = u[:, lanes]
        h_loc = jnp.zeros((SCAN_SEGS, LANES), jnp.float32)
        p_loc = jnp.ones((SCAN_SEGS, LANES), jnp.float32)
        for i in range(seg_len):
            a_i = sa_ref[g, pl.ds(i, SCAN_SEGS, stride=seg_len), :]
            u_i = su_ref[g, pl.ds(i, SCAN_SEGS, stride=seg_len), :]
            h_loc = a_i * h_loc + u_i
            p_loc = a_i * p_loc
            sh_ref[g, pl.ds(i, SCAN_SEGS, stride=seg_len), :] = h_loc
            sp_ref[g, pl.ds(i, SCAN_SEGS, stride=seg_len), :] = p_loc
        carry = hc_ref[0:1, lanes]
        segs = []
        for s in range(SCAN_SEGS):
            rows = slice(s * seg_len, (s + 1) * seg_len)
            segs.append(sh_ref[g, rows, :] + sp_ref[g, rows, :] * carry)
            carry = h_loc[s:s + 1, :] + p_loc[s:s + 1, :] * carry
        hc_ref[0:1, lanes] = carry
        h_groups.append(jnp.concatenate(segs, axis=0))
    h = jnp.concatenate(h_groups, axis=1)
    y = _bdot(h * jax.nn.gelu(p_a[:, BRANCH_W:]), wao_ref[...])
    merged = gate(0) * y

    p_b = proj(OFF_B, 3 * BRANCH_W)
    cv_buf[CONV_A_HALO:CONV_A_HALO + ts, :] = p_b[:, BRANCH_W:2 * BRANCH_W] * p_b[:, :BRANCH_W]
    cb = _causal_conv(cv_buf, cbw_ref, CONV_A_HALO, SC_CONV, ts)
    cv_buf[0:CONV_A_HALO, :] = cv_buf[ts:ts + CONV_A_HALO, :]
    y = _bdot(p_b[:, 2 * BRANCH_W:] * cb, wbo_ref[...])
    merged = merged + gate(1) * y

    p_c = proj(OFF_C, BRANCH_W + 2 * N_KV_HEADS * HEAD_DIM)
    lane = lax.broadcasted_iota(jnp.int32, (ts, LANES), 1)
    low = lane < HEAD_DIM
    k = p_c[:, BRANCH_W:BRANCH_W + LANES]
    v = p_c[:, BRANCH_W + LANES:BRANCH_W + 2 * LANES]
    k_sw = pltpu.roll(k, HEAD_DIM, axis=1)
    v_sw = pltpu.roll(v, HEAD_DIM, axis=1)
    zero = jnp.zeros_like(v)
    cur = slice(BLOCK, BLOCK + ts)
    kv_buf[0, cur, :] = jnp.where(low, k, k_sw).astype(jnp.bfloat16)
    kv_buf[1, cur, :] = jnp.where(low, k_sw, k).astype(jnp.bfloat16)
    kv_buf[2, cur, :] = jnp.where(low, v, zero).astype(jnp.bfloat16)
    kv_buf[3, cur, :] = jnp.where(low, zero, v_sw).astype(jnp.bfloat16)
    kv_buf[4, cur, :] = jnp.where(low, v_sw, zero).astype(jnp.bfloat16)
    kv_buf[5, cur, :] = jnp.where(low, zero, v).astype(jnp.bfloat16)

    qi = lax.broadcasted_iota(jnp.int32, (BLOCK, 2 * BLOCK), 0)
    ki = lax.broadcasted_iota(jnp.int32, (BLOCK, 2 * BLOCK), 1)
    dist = qi + BLOCK - ki
    in_window = (dist >= 0) & (dist < WINDOW)
    dist_f = dist.astype(jnp.float32)
    low_q = lax.broadcasted_iota(jnp.int32, (BLOCK, LANES), 1) < HEAD_DIM
    blocks_out = []
    for blk in range(ts // BLOCK):
        key_pos = ki + (j * ts + (blk - 1) * BLOCK)
        valid = in_window & (key_pos >= 0) if blk == 0 else in_window
        keys = slice(blk * BLOCK, (blk + 2) * BLOCK)
        pair_out = []
        for pair in range(N_Q_HEADS // 2):
            kvh = (2 * pair) // GQA_GROUP
            q2 = p_c[blk * BLOCK:(blk + 1) * BLOCK, pair * LANES:(pair + 1) * LANES] * (HEAD_DIM ** -0.5)
            k2 = kv_buf[kvh, keys, :]
            acc = None
            for half in range(2):
                head = 2 * pair + half
                qh = jnp.where(low_q if half == 0 else ~low_q, q2, 0.0).astype(jnp.bfloat16)
                s = lax.dot_general(qh, k2, (((1,), (1,)), ((), ())), preferred_element_type=jnp.float32)
                s = jnp.where(valid, s - ALIBI_SLOPES[head] * dist_f, NEG_INF)
                sink = sinks_ref[head]
                m = jnp.maximum(jnp.max(s, axis=-1, keepdims=True), sink)
                e = jnp.exp(s - m)
                denom = jnp.sum(e, axis=-1, keepdims=True) + jnp.exp(sink - m)
                prob = (e / denom).astype(jnp.bfloat16)
                o = jnp.dot(prob, kv_buf[2 + 2 * kvh + half, keys, :], preferred_element_type=jnp.float32)
                acc = o if acc is None else acc + o
            pair_out.append(acc)
        blocks_out.append(jnp.concatenate(pair_out, axis=1))
    attn = jnp.concatenate(blocks_out, axis=0)
    kv_buf[:, 0:BLOCK, :] = kv_buf[:, ts:ts + BLOCK, :]
    y = _bdot(attn, wco_ref[...])
    merged = merged + gate(2) * y

    p_d = proj(OFF_D, 2 * BRANCH_W)
    d_buf[CONV_D_HALO:CONV_D_HALO + ts, :] = p_d[:, :BRANCH_W] * jax.nn.sigmoid(p_d[:, BRANCH_W:])
    cd = _causal_conv(d_buf, cdw_ref, CONV_D_HALO, CF_CONV, ts) + cdb_ref[...]
    d_buf[0:CONV_D_HALO, :] = d_buf[ts:ts + CONV_D_HALO, :]
    mu = jnp.mean(cd, axis=-1, keepdims=True)
    var = jnp.mean(jnp.square(cd - mu), axis=-1, keepdims=True)
    ln = (cd - mu) * lax.rsqrt(var + EPS) * lng_ref[...] + lnb_ref[...]
    y = _bdot(jax.nn.silu(ln), wdo_ref[...])
    merged = merged + gate(3) * y

    o_ref[0] = x + _bdot(merged, wo_ref[...])


def _ffn_kernel(x_ref, g_ref, wg_ref, wu_ref, wd_ref, fg_ref, o_ref, *, final_norm):
    x = x_ref[...]
    xn = _rms(x, g_ref[...]).astype(jnp.bfloat16)
    gte = jnp.dot(xn, wg_ref[...], preferred_element_type=jnp.float32)
    up = jnp.dot(xn, wu_ref[...], preferred_element_type=jnp.float32)
    y = x + _bdot(jax.nn.silu(gte) * up, wd_ref[...])
    if final_norm:
        y = _rms(y, fg_ref[...])
    o_ref[...] = y


def _resident(shape):
    return pl.BlockSpec(shape, lambda *_: (0,) * len(shape), pipeline_mode=pl.Buffered(1))


def _mixer(x, sinks, params):
    b, s, d = x.shape
    x_spec = pl.BlockSpec((1, TS, d), lambda i, j: (i, j, 0))
    in_specs = [pl.BlockSpec(memory_space=pltpu.SMEM), x_spec] + [_resident(p.shape) for p in params]
    f32 = jnp.float32
    scratch = [
        pltpu.VMEM((CONV_A_HALO + TS, BRANCH_W), f32),
        pltpu.VMEM((CONV_A_HALO + TS, BRANCH_W), f32),
        pltpu.VMEM((CONV_D_HALO + TS, BRANCH_W), f32),
        pltpu.VMEM((6, BLOCK + TS, LANES), jnp.bfloat16),
        pltpu.VMEM((SUBLANES, BRANCH_W), f32),
    ] + [pltpu.VMEM((BRANCH_W // LANES, TS, LANES), f32)] * 4
    return pl.pallas_call(
        _mixer_kernel,
        grid=(b, s // TS),
        in_specs=in_specs,
        out_specs=x_spec,
        out_shape=jax.ShapeDtypeStruct(x.shape, x.dtype),
        scratch_shapes=scratch,
        compiler_params=pltpu.CompilerParams(
            dimension_semantics=("arbitrary", "arbitrary"), vmem_limit_bytes=VMEM_LIMIT_MIXER),
        name="mixer",
    )(sinks, x, *params)


def _ffn(x2, g, wg, wu, wd, fg, final_norm):
    t, d = x2.shape
    x_spec = pl.BlockSpec((TM, d), lambda i: (i, 0))
    return pl.pallas_call(
        functools.partial(_ffn_kernel, final_norm=final_norm),
        grid=(t // TM,),
        in_specs=[x_spec] + [_resident(p.shape) for p in (g, wg, wu, wd, fg)],
        out_specs=x_spec,
        out_shape=jax.ShapeDtypeStruct(x2.shape, x2.dtype),
        compiler_params=pltpu.CompilerParams(
            dimension_semantics=("arbitrary",), vmem_limit_bytes=VMEM_LIMIT_FFN),
        name="ffn",
    )(x2, g, wg, wu, wd, fg)


def _block_diag(w):
    h, wb, _ = w.shape
    eye = jnp.eye(h, dtype=w.dtype)
    return (w[:, :, None, :] * eye[:, None, :, None]).reshape(h * wb, h * wb)


def kernel(x, norm1_g, w_in, conv_a_w, conv_a_b, lru_wx, lru_bx, lru_wa, lru_ba, lru_lambda, w_a_out, conv_b_w, w_b_out, sinks, w_c_out, conv_d_w, conv_d_b, ln_d_g, ln_d_b, w_d_out, w_o, norm2_g, w_ffn_gate, w_ffn_up, w_ffn_down, final_g):
    b, s, d = x.shape
    depth = w_in.shape[0]
    bf16 = jnp.bfloat16
    row = lambda a: a.reshape(1, -1)
    for l in range(depth):
        params = (
            row(norm1_g[l]), w_in[l].astype(bf16), conv_a_w[l], row(conv_a_b[l]),
            _block_diag(lru_wx[l]).astype(bf16), row(lru_bx[l]), _block_diag(lru_wa[l]).astype(bf16),
            row(lru_ba[l]), row(lru_lambda[l]), w_a_out[l].astype(bf16), conv_b_w[l], w_b_out[l].astype(bf16),
            w_c_out[l].astype(bf16), conv_d_w[l], row(conv_d_b[l]), row(ln_d_g[l]), row(ln_d_b[l]),
            w_d_out[l].astype(bf16), w_o[l].astype(bf16),
        )
        x = _mixer(x, sinks[l], params)
        x = _ffn(x.reshape(b * s, d), row(norm2_g[l]), w_ffn_gate[l].astype(bf16), w_ffn_up[l].astype(bf16),
                 w_ffn_down[l].astype(bf16), row(final_g), l == depth - 1).reshape(b, s, d)
    return x
```

```python
import functools

import jax
import jax.numpy as jnp
from jax import lax
from jax.experimental import pallas as pl
from jax.experimental.pallas import tpu as pltpu

D_MODEL = 1024
BRANCH_W = 512
LRU_CONV = 4
LRU_C = 8.0
SC_CONV = 3
HEAD_DIM = 64
N_Q_HEADS = 8
N_KV_HEADS = 2
GQA_GROUP = N_Q_HEADS // N_KV_HEADS
WINDOW = 128
BLOCK = 128
CF_CONV = 31
N_BRANCHES = 4
EPS = 1e-6
NEG_INF = -1e30

OFF_A = 0
OFF_B = OFF_A + 2 * BRANCH_W
OFF_C = OFF_B + 3 * BRANCH_W
OFF_D = OFF_C + BRANCH_W + 2 * N_KV_HEADS * HEAD_DIM
OFF_G = OFF_D + 2 * BRANCH_W

SUBLANES = 8
LANES = 128
CONV_A_HALO = 8
CONV_D_HALO = 32
CONV_D_ROWS = 64
TS = 256
TM = 512
VMEM_LIMIT_MIXER = 56 * 1024 * 1024
VMEM_LIMIT_FFN = 48 * 1024 * 1024

ALIBI_SLOPES = tuple(2.0 ** (-8.0 * (i + 1) / N_Q_HEADS) for i in range(N_Q_HEADS))


def _rms(x, g):
    return x * lax.rsqrt(jnp.mean(x * x, axis=-1, keepdims=True) + EPS) * g


def _bdot(a, b):
    return jnp.dot(a.astype(jnp.bfloat16), b, preferred_element_type=jnp.float32)


def _causal_conv(buf_ref, w_ref, halo, width, rows):
    acc = None
    for k in range(width):
        term = w_ref[k:k + 1, :] * buf_ref[pl.ds(halo - (width - 1) + k, rows), :]
        acc = term if acc is None else acc + term
    return acc


def _causal_conv_wide(buf_ref, w_ref, out_ref, halo, width, rows):
    n_a = -(-width // SUBLANES)
    assert halo >= SUBLANES * n_a and rows % CONV_D_ROWS == 0
    ext = CONV_D_ROWS + SUBLANES
    for g in range(BRANCH_W // LANES):
        lanes = slice(g * LANES, (g + 1) * LANES)
        for t0 in range(0, rows, CONV_D_ROWS):
            slabs = [buf_ref[pl.ds(halo + t0 - SUBLANES * (a + 1), ext), lanes] for a in range(n_a)]
            y = None
            for r in range(SUBLANES):
                z = None
                for a in range(n_a):
                    k = width - 1 - (SUBLANES * a + r)
                    if k < 0:
                        continue
                    term = w_ref[k:k + 1, lanes] * slabs[a]
                    z = term if z is None else z + term
                shifted = z[SUBLANES - r:SUBLANES - r + CONV_D_ROWS, :]
                y = shifted if y is None else y + shifted
            out_ref[t0:t0 + CONV_D_ROWS, lanes] = y


def _mixer_kernel(sinks_ref, x_ref, n1g_ref, w_in_ref, caw_ref, cab_ref, wx_ref, bx_ref, wa_ref, ba_ref,
                  lam_ref, wao_ref, cbw_ref, wbo_ref, wco_ref, cdw_ref, cdb_ref, lng_ref, lnb_ref, wdo_ref,
                  wo_ref, o_ref, xn_s, ax_buf, cv_buf, d_buf, cd_buf, q_s, kv_buf, hc_ref):
    j = pl.program_id(1)
    ts = x_ref.shape[1]
    n_blk = ts // BLOCK
    bf16 = jnp.bfloat16
    f32 = jnp.float32
    cur_a = slice(CONV_A_HALO, CONV_A_HALO + ts)

    def proj(off, width):
        return jnp.dot(xn_s[...], w_in_ref[:, off:off + width], preferred_element_type=f32)

    def gate(idx):
        return jax.nn.sigmoid(proj(OFF_G + idx * D_MODEL, D_MODEL))

    def out_proj(act, w_ref):
        return jnp.dot(act, w_ref[...], preferred_element_type=f32)

    @pl.when(j == 0)
    def _():
        ax_buf[0:CONV_A_HALO, :] = jnp.zeros((CONV_A_HALO, BRANCH_W), f32)
        cv_buf[0:CONV_A_HALO, :] = jnp.zeros((CONV_A_HALO, BRANCH_W), f32)
        d_buf[0:CONV_D_HALO, :] = jnp.zeros((CONV_D_HALO, BRANCH_W), f32)
        kv_buf[:, 0:BLOCK, :] = jnp.zeros((6, BLOCK, LANES), bf16)
        hc_ref[...] = jnp.zeros_like(hc_ref)

    x = x_ref[0]
    xn_s[...] = _rms(x, n1g_ref[...]).astype(bf16)

    p_a = proj(OFF_A, 2 * BRANCH_W)
    ax_buf[cur_a, :] = p_a[:, :BRANCH_W]
    c = _causal_conv(ax_buf, caw_ref, CONV_A_HALO, LRU_CONV, ts) + cab_ref[...]
    ax_buf[0:CONV_A_HALO, :] = ax_buf[ts:ts + CONV_A_HALO, :]

    p_d = proj(OFF_D, 2 * BRANCH_W)
    d_buf[CONV_D_HALO:CONV_D_HALO + ts, :] = p_d[:, :BRANCH_W] * jax.nn.sigmoid(p_d[:, BRANCH_W:])

    gate_i = jax.nn.sigmoid(_bdot(c, wx_ref[...]) + bx_ref[...])
    gate_r = jax.nn.sigmoid(_bdot(c, wa_ref[...]) + ba_ref[...])
    z = -lam_ref[...]
    softplus = jnp.maximum(z, 0.0) + jnp.log1p(jnp.exp(-jnp.abs(z)))
    log_a = -LRU_C * gate_r * softplus
    a = jnp.exp(log_a)
    u = (c * gate_i) * jnp.sqrt(-jnp.tanh(log_a) * (a * a + 1.0))
    n_grp = ts // SUBLANES
    a3 = a.reshape(n_grp, SUBLANES, BRANCH_W)
    u3 = u.reshape(n_grp, SUBLANES, BRANCH_W)
    sub = lax.broadcasted_iota(jnp.int32, (n_grp, SUBLANES, BRANCH_W), 1)
    for step in (1, 2, 4):
        keep = sub >= step
        u3 = a3 * jnp.where(keep, pltpu.roll(u3, step, axis=1), 0.0) + u3
        a3 = a3 * jnp.where(keep, pltpu.roll(a3, step, axis=1), 1.0)
    carry = hc_ref[0:1, :]
    h_rows = []
    for i in range(n_grp):
        h_i = u3[i] + a3[i] * carry
        h_rows.append(h_i)
        carry = h_i[SUBLANES - 1:SUBLANES, :]
    hc_ref[0:1, :] = carry
    act_a = (jnp.concatenate(h_rows, axis=0) * jax.nn.gelu(p_a[:, BRANCH_W:])).astype(bf16)

    p_c = proj(OFF_C, BRANCH_W + 2 * N_KV_HEADS * HEAD_DIM)
    q_s[...] = (p_c[:, :BRANCH_W] * (HEAD_DIM ** -0.5)).astype(bf16)
    low = lax.broadcasted_iota(jnp.int32, (ts, LANES), 1) < HEAD_DIM
    k = p_c[:, BRANCH_W:BRANCH_W + LANES]
    v = p_c[:, BRANCH_W + LANES:BRANCH_W + 2 * LANES]
    k_sw = pltpu.roll(k, HEAD_DIM, axis=1)
    v_sw = pltpu.roll(v, HEAD_DIM, axis=1)
    zero = jnp.zeros_like(v)
    cur = slice(BLOCK, BLOCK + ts)
    kv_buf[0, cur, :] = jnp.where(low, k, k_sw).astype(bf16)
    kv_buf[1, cur, :] = jnp.where(low, k_sw, k).astype(bf16)
    kv_buf[2, cur, :] = jnp.where(low, v, zero).astype(bf16)
    kv_buf[3, cur, :] = jnp.where(low, zero, v_sw).astype(bf16)
    kv_buf[4, cur, :] = jnp.where(low, v_sw, zero).astype(bf16)
    kv_buf[5, cur, :] = jnp.where(low, zero, v).astype(bf16)
    low_q = lax.broadcasted_iota(jnp.int32, (BLOCK, LANES), 1) < HEAD_DIM
    scores = {}
    for blk in range(n_blk):
        keys = slice(blk * BLOCK, (blk + 2) * BLOCK)
        for pair in range(N_Q_HEADS // 2):
            kvh = (2 * pair) // GQA_GROUP
            q2 = q_s[blk * BLOCK:(blk + 1) * BLOCK, pair * LANES:(pair + 1) * LANES]
            zq = jnp.zeros_like(q2)
            qq = jnp.concatenate([jnp.where(low_q, q2, zq), jnp.where(low_q, zq, q2)], axis=0)
            scores[blk, pair] = lax.dot_general(qq, kv_buf[kvh, keys, :], (((1,), (1,)), ((), ())),
                                                preferred_element_type=f32)

    p_b = proj(OFF_B, 3 * BRANCH_W)
    cv_buf[cur_a, :] = p_b[:, BRANCH_W:2 * BRANCH_W] * p_b[:, :BRANCH_W]
    cb = _causal_conv(cv_buf, cbw_ref, CONV_A_HALO, SC_CONV, ts)
    cv_buf[0:CONV_A_HALO, :] = cv_buf[ts:ts + CONV_A_HALO, :]
    act_b = (p_b[:, 2 * BRANCH_W:] * cb).astype(bf16)
    gate_d = gate(3)
    gate_a = gate(0)

    _causal_conv_wide(d_buf, cdw_ref, cd_buf, CONV_D_HALO, CF_CONV, ts)
    d_buf[0:CONV_D_HALO, :] = d_buf[ts:ts + CONV_D_HALO, :]
    cd = cd_buf[...] + cdb_ref[...]
    mu = jnp.mean(cd, axis=-1, keepdims=True)
    var = jnp.mean(jnp.square(cd - mu), axis=-1, keepdims=True)
    ln = (cd - mu) * lax.rsqrt(var + EPS) * lng_ref[...] + lnb_ref[...]
    act_d = jax.nn.silu(ln).astype(bf16)
    merged = gate_d * out_proj(act_d, wdo_ref)
    gate_b = gate(1)

    qi = lax.broadcasted_iota(jnp.int32, (BLOCK, 2 * BLOCK), 0)
    ki = lax.broadcasted_iota(jnp.int32, (BLOCK, 2 * BLOCK), 1)
    dist = qi + BLOCK - ki
    in_window = (dist >= 0) & (dist < WINDOW)
    dist_f = dist.astype(f32)
    probs = {}
    for blk in range(n_blk):
        key_pos = ki + (j * ts + (blk - 1) * BLOCK)
        valid = in_window & (key_pos >= 0) if blk == 0 else in_window
        for pair in range(N_Q_HEADS // 2):
            both = []
            for half in range(2):
                head = 2 * pair + half
                s = scores[blk, pair][half * BLOCK:(half + 1) * BLOCK, :]
                s = jnp.where(valid, s - ALIBI_SLOPES[head] * dist_f, NEG_INF)
                sink = sinks_ref[head]
                m = jnp.maximum(jnp.max(s, axis=-1, keepdims=True), sink)
                e = jnp.exp(s - m)
                denom = jnp.sum(e, axis=-1, keepdims=True) + jnp.exp(sink - m)
                both.append((e / denom).astype(bf16))
            probs[blk, pair] = jnp.concatenate(both, axis=1)
    merged = merged + gate_a * out_proj(act_a, wao_ref)
    gate_c = gate(2)
    blocks_out = []
    for blk in range(n_blk):
        keys = slice(blk * BLOCK, (blk + 2) * BLOCK)
        pair_out = []
        for pair in range(N_Q_HEADS // 2):
            kvh = (2 * pair) // GQA_GROUP
            vv = jnp.concatenate([kv_buf[2 + 2 * kvh, keys, :], kv_buf[3 + 2 * kvh, keys, :]], axis=0)
            pair_out.append(jnp.dot(probs[blk, pair], vv, preferred_element_type=f32))
        blocks_out.append(jnp.concatenate(pair_out, axis=1))
    act_c = jnp.concatenate(blocks_out, axis=0).astype(bf16)
    kv_buf[:, 0:BLOCK, :] = kv_buf[:, ts:ts + BLOCK, :]

    merged = merged + gate_b * out_proj(act_b, wbo_ref)
    merged = merged + gate_c * out_proj(act_c, wco_ref)
    o_ref[0] = x + _bdot(merged, wo_ref[...])


def _ffn_kernel(x_ref, g_ref, wg_ref, wu_ref, wd_ref, fg_ref, o_ref, *, final_norm):
    x = x_ref[...]
    xn = _rms(x, g_ref[...]).astype(jnp.bfloat16)
    gte = jnp.dot(xn, wg_ref[...], preferred_element_type=jnp.float32)
    up = jnp.dot(xn, wu_ref[...], preferred_element_type=jnp.float32)
    y = x + _bdot(jax.nn.silu(gte) * up, wd_ref[...])
    if final_norm:
        y = _rms(y, fg_ref[...])
    o_ref[...] = y


def _resident(shape):
    return pl.BlockSpec(shape, lambda *_: (0,) * len(shape), pipeline_mode=pl.Buffered(1))


def _mixer(x, sinks, params):
    b, s, d = x.shape
    x_spec = pl.BlockSpec((1, TS, d), lambda i, j: (i, j, 0))
    in_specs = [pl.BlockSpec(memory_space=pltpu.SMEM), x_spec] + [_resident(p.shape) for p in params]
    f32, bf16 = jnp.float32, jnp.bfloat16
    scratch = [
        pltpu.VMEM((TS, d), bf16),
        pltpu.VMEM((CONV_A_HALO + TS, BRANCH_W), f32),
        pltpu.VMEM((CONV_A_HALO + TS, BRANCH_W), f32),
        pltpu.VMEM((CONV_D_HALO + TS, BRANCH_W), f32),
        pltpu.VMEM((TS, BRANCH_W), f32),
        pltpu.VMEM((TS, BRANCH_W), bf16),
        pltpu.VMEM((6, BLOCK + TS, LANES), bf16),
        pltpu.VMEM((SUBLANES, BRANCH_W), f32),
    ]
    return pl.pallas_call(
        _mixer_kernel,
        grid=(b, s // TS),
        in_specs=in_specs,
        out_specs=x_spec,
        out_shape=jax.ShapeDtypeStruct(x.shape, x.dtype),
        scratch_shapes=scratch,
        compiler_params=pltpu.CompilerParams(
            dimension_semantics=("arbitrary", "arbitrary"), vmem_limit_bytes=VMEM_LIMIT_MIXER),
        name="mixer",
    )(sinks, x, *params)


def _ffn(x2, g, wg, wu, wd, fg, final_norm):
    t, d = x2.shape
    x_spec = pl.BlockSpec((TM, d), lambda i: (i, 0))
    return pl.pallas_call(
        functools.partial(_ffn_kernel, final_norm=final_norm),
        grid=(t // TM,),
        in_specs=[x_spec] + [_resident(p.shape) for p in (g, wg, wu, wd, fg)],
        out_specs=x_spec,
        out_shape=jax.ShapeDtypeStruct(x2.shape, x2.dtype),
        compiler_params=pltpu.CompilerParams(
            dimension_semantics=("arbitrary",), vmem_limit_bytes=VMEM_LIMIT_FFN),
        name="ffn",
    )(x2, g, wg, wu, wd, fg)


def _block_diag(w):
    h, wb, _ = w.shape
    eye = jnp.eye(h, dtype=w.dtype)
    return (w[:, :, None, :] * eye[:, None, :, None]).reshape(h * wb, h * wb)


def kernel(x, norm1_g, w_in, conv_a_w, conv_a_b, lru_wx, lru_bx, lru_wa, lru_ba, lru_lambda, w_a_out, conv_b_w, w_b_out, sinks, w_c_out, conv_d_w, conv_d_b, ln_d_g, ln_d_b, w_d_out, w_o, norm2_g, w_ffn_gate, w_ffn_up, w_ffn_down, final_g):
    b, s, d = x.shape
    depth = w_in.shape[0]
    bf16 = jnp.bfloat16
    row = lambda a: a.reshape(1, -1)
    for l in range(depth):
        params = (
            row(norm1_g[l]), w_in[l].astype(bf16), conv_a_w[l], row(conv_a_b[l]),
            _block_diag(lru_wx[l]).astype(bf16), row(lru_bx[l]), _block_diag(lru_wa[l]).astype(bf16),
            row(lru_ba[l]), row(lru_lambda[l]), w_a_out[l].astype(bf16), conv_b_w[l], w_b_out[l].astype(bf16),
            w_c_out[l].astype(bf16), conv_d_w[l], row(conv_d_b[l]), row(ln_d_g[l]), row(ln_d_b[l]),
            w_d_out[l].astype(bf16), w_o[l].astype(bf16),
        )
        x = _mixer(x, sinks[l], params)
        x = _ffn(x.reshape(b * s, d), row(norm2_g[l]), w_ffn_gate[l].astype(bf16), w_ffn_up[l].astype(bf16),
                 w_ffn_down[l].astype(bf16), row(final_g), l == depth - 1).reshape(b, s, d)
    return x
```

```python
import functools

import jax
import jax.numpy as jnp
from jax import lax
from jax.experimental import pallas as pl
from jax.experimental.pallas import tpu as pltpu

D_MODEL = 1024
BRANCH_W = 512
LRU_CONV = 4
LRU_C = 8.0
SC_CONV = 3
HEAD_DIM = 64
N_Q_HEADS = 8
N_KV_HEADS = 2
GQA_GROUP = N_Q_HEADS // N_KV_HEADS
WINDOW = 128
BLOCK = 128
CF_CONV = 31
N_BRANCHES = 4
EPS = 1e-6
NEG_INF = -1e30

OFF_A = 0
OFF_B = OFF_A + 2 * BRANCH_W
OFF_C = OFF_B + 3 * BRANCH_W
OFF_D = OFF_C + BRANCH_W + 2 * N_KV_HEADS * HEAD_DIM
OFF_G = OFF_D + 2 * BRANCH_W

SUBLANES = 8
LANES = 128
CONV_A_HALO = 8
CONV_D_HALO = 32
CONV_D_ROWS = 64
TS = 256
TM = 512
VMEM_LIMIT_MIXER = 56 * 1024 * 1024
VMEM_LIMIT_FFN = 48 * 1024 * 1024

ALIBI_SLOPES = tuple(2.0 ** (-8.0 * (i + 1) / N_Q_HEADS) for i in range(N_Q_HEADS))


def _rms(x, g):
    return x * lax.rsqrt(jnp.mean(x * x, axis=-1, keepdims=True) + EPS) * g


def _bdot(a, b):
    return jnp.dot(a.astype(jnp.bfloat16), b, preferred_element_type=jnp.float32)


def _causal_conv(buf_ref, w_ref, halo, width, rows):
    acc = None
    for k in range(width):
        term = w_ref[k:k + 1, :] * buf_ref[pl.ds(halo - (width - 1) + k, rows), :]
        acc = term if acc is None else acc + term
    return acc


def _causal_conv_wide(buf_ref, w_ref, out_ref, halo, width, rows):
    n_a = -(-width // SUBLANES)
    assert halo >= SUBLANES * n_a and rows % CONV_D_ROWS == 0
    ext = CONV_D_ROWS + SUBLANES
    for g in range(BRANCH_W // LANES):
        lanes = slice(g * LANES, (g + 1) * LANES)
        for t0 in range(0, rows, CONV_D_ROWS):
            slabs = [buf_ref[pl.ds(halo + t0 - SUBLANES * (a + 1), ext), lanes] for a in range(n_a)]
            y = None
            for r in range(SUBLANES):
                z = None
                for a in range(n_a):
                    k = width - 1 - (SUBLANES * a + r)
                    if k < 0:
                        continue
                    term = w_ref[k:k + 1, lanes] * slabs[a]
                    z = term if z is None else z + term
                shifted = z[SUBLANES - r:SUBLANES - r + CONV_D_ROWS, :]
                y = shifted if y is None else y + shifted
            out_ref[t0:t0 + CONV_D_ROWS, lanes] = y


def _mixer_kernel(sinks_ref, x_ref, n1g_ref, w_in_ref, caw_ref, cab_ref, wx_ref, bx_ref, wa_ref, ba_ref,
                  lam_ref, wao_ref, cbw_ref, wbo_ref, wco_ref, cdw_ref, cdb_ref, lng_ref, lnb_ref, wdo_ref,
                  wo_ref, o_ref, xn_s, ax_buf, cv_buf, d_buf, cd_buf, q_s, kv_buf, hc_ref):
    j = pl.program_id(1)
    ts = x_ref.shape[1]
    n_blk = ts // BLOCK
    bf16 = jnp.bfloat16
    f32 = jnp.float32
    cur_a = slice(CONV_A_HALO, CONV_A_HALO + ts)

    def proj(off, width):
        return jnp.dot(xn_s[...], w_in_ref[:, off:off + width], preferred_element_type=f32)

    def gate(idx):
        return jax.nn.sigmoid(proj(OFF_G + idx * D_MODEL, D_MODEL))

    def out_proj(act, w_ref):
        return jnp.dot(act, w_ref[...], preferred_element_type=f32)

    @pl.when(j == 0)
    def _():
        ax_buf[0:CONV_A_HALO, :] = jnp.zeros((CONV_A_HALO, BRANCH_W), f32)
        cv_buf[0:CONV_A_HALO, :] = jnp.zeros((CONV_A_HALO, BRANCH_W), f32)
        d_buf[0:CONV_D_HALO, :] = jnp.zeros((CONV_D_HALO, BRANCH_W), f32)
        kv_buf[:, 0:BLOCK, :] = jnp.zeros((6, BLOCK, LANES), bf16)
        hc_ref[...] = jnp.zeros_like(hc_ref)

    x = x_ref[0]
    xn_s[...] = _rms(x, n1g_ref[...]).astype(bf16)

    p_a = proj(OFF_A, 2 * BRANCH_W)
    ax_buf[cur_a, :] = p_a[:, :BRANCH_W]
    c = _causal_conv(ax_buf, caw_ref, CONV_A_HALO, LRU_CONV, ts) + cab_ref[...]
    ax_buf[0:CONV_A_HALO, :] = ax_buf[ts:ts + CONV_A_HALO, :]

    p_d = proj(OFF_D, 2 * BRANCH_W)
    d_buf[CONV_D_HALO:CONV_D_HALO + ts, :] = p_d[:, :BRANCH_W] * jax.nn.sigmoid(p_d[:, BRANCH_W:])

    gate_i = jax.nn.sigmoid(_bdot(c, wx_ref[...]) + bx_ref[...])
    gate_r = jax.nn.sigmoid(_bdot(c, wa_ref[...]) + ba_ref[...])
    z = -lam_ref[...]
    softplus = jnp.maximum(z, 0.0) + jnp.log1p(jnp.exp(-jnp.abs(z)))
    log_a = -LRU_C * gate_r * softplus
    a = jnp.exp(log_a)
    u = (c * gate_i) * jnp.sqrt(-jnp.tanh(log_a) * (a * a + 1.0))
    n_grp = ts // SUBLANES
    a3 = a.reshape(n_grp, SUBLANES, BRANCH_W)
    u3 = u.reshape(n_grp, SUBLANES, BRANCH_W)
    sub = lax.broadcasted_iota(jnp.int32, (n_grp, SUBLANES, BRANCH_W), 1)
    for step in (1, 2, 4):
        keep = sub >= step
        u3 = a3 * jnp.where(keep, pltpu.roll(u3, step, axis=1), 0.0) + u3
        a3 = a3 * jnp.where(keep, pltpu.roll(a3, step, axis=1), 1.0)
    carry = hc_ref[0:1, :]
    h_rows = []
    for i in range(n_grp):
        h_i = u3[i] + a3[i] * carry
        h_rows.append(h_i)
        carry = h_i[SUBLANES - 1:SUBLANES, :]
    hc_ref[0:1, :] = carry
    act_a = (jnp.concatenate(h_rows, axis=0) * jax.nn.gelu(p_a[:, BRANCH_W:])).astype(bf16)

    p_c = proj(OFF_C, BRANCH_W + 2 * N_KV_HEADS * HEAD_DIM)
    q_s[...] = (p_c[:, :BRANCH_W] * (HEAD_DIM ** -0.5)).astype(bf16)
    low = lax.broadcasted_iota(jnp.int32, (ts, LANES), 1) < HEAD_DIM
    k = p_c[:, BRANCH_W:BRANCH_W + LANES]
    v = p_c[:, BRANCH_W + LANES:BRANCH_W + 2 * LANES]
    k_sw = pltpu.roll(k, HEAD_DIM, axis=1)
    v_sw = pltpu.roll(v, HEAD_DIM, axis=1)
    zero = jnp.zeros_like(v)
    cur = slice(BLOCK, BLOCK + ts)
    kv_buf[0, cur, :] = jnp.where(low, k, k_sw).astype(bf16)
    kv_buf[1, cur, :] = jnp.where(low, k_sw, k).astype(bf16)
    kv_buf[2, cur, :] = jnp.where(low, v, zero).astype(bf16)
    kv_buf[3, cur, :] = jnp.where(low, zero, v_sw).astype(bf16)
    kv_buf[4, cur, :] = jnp.where(low, v_sw, zero).astype(bf16)
    kv_buf[5, cur, :] = jnp.where(low, zero, v).astype(bf16)
    low_q = lax.broadcasted_iota(jnp.int32, (BLOCK, LANES), 1) < HEAD_DIM
    scores = {}
    for blk in range(n_blk):
        keys = slice(blk * BLOCK, (blk + 2) * BLOCK)
        for pair in range(N_Q_HEADS // 2):
            kvh = (2 * pair) // GQA_GROUP
            q2 = q_s[blk * BLOCK:(blk + 1) * BLOCK, pair * LANES:(pair + 1) * LANES]
            zq = jnp.zeros_like(q2)
            qq = jnp.concatenate([jnp.where(low_q, q2, zq), jnp.where(low_q, zq, q2)], axis=0)
            scores[blk, pair] = lax.dot_general(qq, kv_buf[kvh, keys, :], (((1,), (1,)), ((), ())),
                                                preferred_element_type=f32)

    p_b = proj(OFF_B, 3 * BRANCH_W)
    cv_buf[cur_a, :] = p_b[:, BRANCH_W:2 * BRANCH_W] * p_b[:, :BRANCH_W]
    cb = _causal_conv(cv_buf, cbw_ref, CONV_A_HALO, SC_CONV, ts)
    cv_buf[0:CONV_A_HALO, :] = cv_buf[ts:ts + CONV_A_HALO, :]
    act_b = (p_b[:, 2 * BRANCH_W:] * cb).astype(bf16)
    gate_d = gate(3)
    gate_a = gate(0)

    _causal_conv_wide(d_buf, cdw_ref, cd_buf, CONV_D_HALO, CF_CONV, ts)
    d_buf[0:CONV_D_HALO, :] = d_buf[ts:ts + CONV_D_HALO, :]
    cd = cd_buf[...] + cdb_ref[...]
    mu = jnp.mean(cd, axis=-1, keepdims=True)
    var = jnp.mean(jnp.square(cd - mu), axis=-1, keepdims=True)
    ln = (cd - mu) * lax.rsqrt(var + EPS) * lng_ref[...] + lnb_ref[...]
    act_d = jax.nn.silu(ln).astype(bf16)
    merged = gate_d * out_proj(act_d, wdo_ref)
    gate_b = gate(1)

    qi = lax.broadcasted_iota(jnp.int32, (BLOCK, 2 * BLOCK), 0)
    ki = lax.broadcasted_iota(jnp.int32, (BLOCK, 2 * BLOCK), 1)
    dist = qi + BLOCK - ki
    in_window = (dist >= 0) & (dist < WINDOW)
    dist_f = dist.astype(f32)
    probs = {}
    for blk in range(n_blk):
        key_pos = ki + (j * ts + (blk - 1) * BLOCK)
        valid = in_window & (key_pos >= 0) if blk == 0 else in_window
        for pair in range(N_Q_HEADS // 2):
            both = []
            for half in range(2):
                head = 2 * pair + half
                s = scores[blk, pair][half * BLOCK:(half + 1) * BLOCK, :]
                s = jnp.where(valid, s - ALIBI_SLOPES[head] * dist_f, NEG_INF)
                sink = sinks_ref[head]
                m = jnp.maximum(jnp.max(s, axis=-1, keepdims=True), sink)
                e = jnp.exp(s - m)
                denom = jnp.sum(e, axis=-1, keepdims=True) + jnp.exp(sink - m)
                both.append((e / denom).astype(bf16))
            probs[blk, pair] = jnp.concatenate(both, axis=1)
    merged = merged + gate_a * out_proj(act_a, wao_ref)
    gate_c = gate(2)
    blocks_out = []
    for blk in range(n_blk):
        keys = slice(blk * BLOCK, (blk + 2) * BLOCK)
        pair_out = []
        for pair in range(N_Q_HEADS // 2):
            kvh = (2 * pair) // GQA_GROUP
            vv = jnp.concatenate([kv_buf[2 + 2 * kvh, keys, :], kv_buf[3 + 2 * kvh, keys, :]], axis=0)
            pair_out.append(jnp.dot(probs[blk, pair], vv, preferred_element_type=f32))
        blocks_out.append(jnp.concatenate(pair_out, axis=1))
    act_c = jnp.concatenate(blocks_out, axis=0).astype(bf16)
    kv_buf[:, 0:BLOCK, :] = kv_buf[:, ts:ts + BLOCK, :]

    merged = merged + gate_b * out_proj(act_b, wbo_ref)
    merged = merged + gate_c * out_proj(act_c, wco_ref)
    o_ref[0] = x + _bdot(merged, wo_ref[...])


def _ffn_kernel(x_ref, g_ref, wg_ref, wu_ref, wd_ref, fg_ref, o_ref, *, final_norm):
    x = x_ref[...]
    xn = _rms(x, g_ref[...]).astype(jnp.bfloat16)
    gte = jnp.dot(xn, wg_ref[...], preferred_element_type=jnp.float32)
    up = jnp.dot(xn, wu_ref[...], preferred_element_type=jnp.float32)
    y = x + _bdot(jax.nn.silu(gte) * up, wd_ref[...])
    if final_norm:
        y = _rms(y, fg_ref[...])
    o_ref[...] = y


def _layer_spec(arr, layer):
    rest = arr.shape[1:]
    return pl.BlockSpec((None,) + rest, lambda *_: (layer,) + (0,) * len(rest), pipeline_mode=pl.Buffered(1))


def _mixer(x, layer, sinks, params):
    b, s, d = x.shape
    x_spec = pl.BlockSpec((1, TS, d), lambda i, j: (i, j, 0))
    in_specs = [pl.BlockSpec(memory_space=pltpu.SMEM), x_spec] + [_layer_spec(p, layer) for p in params]
    f32, bf16 = jnp.float32, jnp.bfloat16
    scratch = [
        pltpu.VMEM((TS, d), bf16),
        pltpu.VMEM((CONV_A_HALO + TS, BRANCH_W), f32),
        pltpu.VMEM((CONV_A_HALO + TS, BRANCH_W), f32),
        pltpu.VMEM((CONV_D_HALO + TS, BRANCH_W), f32),
        pltpu.VMEM((TS, BRANCH_W), f32),
        pltpu.VMEM((TS, BRANCH_W), bf16),
        pltpu.VMEM((6, BLOCK + TS, LANES), bf16),
        pltpu.VMEM((SUBLANES, BRANCH_W), f32),
    ]
    return pl.pallas_call(
        _mixer_kernel,
        grid=(b, s // TS),
        in_specs=in_specs,
        out_specs=x_spec,
        out_shape=jax.ShapeDtypeStruct(x.shape, x.dtype),
        scratch_shapes=scratch,
        compiler_params=pltpu.CompilerParams(
            dimension_semantics=("arbitrary", "arbitrary"), vmem_limit_bytes=VMEM_LIMIT_MIXER),
        name="mixer",
    )(sinks[layer], x, *params)


def _ffn(x2, layer, params, final_g, final_norm):
    t, d = x2.shape
    x_spec = pl.BlockSpec((TM, d), lambda i: (i, 0))
    fg_spec = pl.BlockSpec(final_g.shape, lambda i: (0, 0), pipeline_mode=pl.Buffered(1))
    return pl.pallas_call(
        functools.partial(_ffn_kernel, final_norm=final_norm),
        grid=(t // TM,),
        in_specs=[x_spec] + [_layer_spec(p, layer) for p in params] + [fg_spec],
        out_specs=x_spec,
        out_shape=jax.ShapeDtypeStruct(x2.shape, x2.dtype),
        compiler_params=pltpu.CompilerParams(
            dimension_semantics=("arbitrary",), vmem_limit_bytes=VMEM_LIMIT_FFN),
        name="ffn",
    )(x2, *params, final_g)


def _block_diag(w):
    l, h, wb, _ = w.shape
    eye = jnp.eye(h, dtype=w.dtype)
    return (w[:, :, :, None, :] * eye[None, :, None, :, None]).reshape(l, h * wb, h * wb)


def kernel(x, norm1_g, w_in, conv_a_w, conv_a_b, lru_wx, lru_bx, lru_wa, lru_ba, lru_lambda, w_a_out, conv_b_w, w_b_out, sinks, w_c_out, conv_d_w, conv_d_b, ln_d_g, ln_d_b, w_d_out, w_o, norm2_g, w_ffn_gate, w_ffn_up, w_ffn_down, final_g):
    b, s, d = x.shape
    depth = w_in.shape[0]
    bf16 = jnp.bfloat16
    rows = lambda a: a.reshape(depth, 1, -1)
    mixer_params = (
        rows(norm1_g), w_in.astype(bf16), conv_a_w, rows(conv_a_b),
        _block_diag(lru_wx).astype(bf16), rows(lru_bx), _block_diag(lru_wa).astype(bf16),
        rows(lru_ba), rows(lru_lambda), w_a_out.astype(bf16), conv_b_w, w_b_out.astype(bf16),
        w_c_out.astype(bf16), conv_d_w, rows(conv_d_b), rows(ln_d_g), rows(ln_d_b),
        w_d_out.astype(bf16), w_o.astype(bf16),
    )
    ffn_params = (rows(norm2_g), w_ffn_gate.astype(bf16), w_ffn_up.astype(bf16), w_ffn_down.astype(bf16))
    for l in range(depth):
        x = _mixer(x, l, sinks, mixer_params)
        x = _ffn(x.reshape(b * s, d), l, ffn_params, final_g.reshape(1, -1), l == depth - 1).reshape(b, s, d)
    return x
```

```python
import functools

import jax
import jax.numpy as jnp
from jax import lax
from jax.experimental import pallas as pl
from jax.experimental.pallas import tpu as pltpu

D_MODEL = 1024
BRANCH_W = 512
LRU_CONV = 4
LRU_C = 8.0
SC_CONV = 3
HEAD_DIM = 64
N_Q_HEADS = 8
N_KV_HEADS = 2
GQA_GROUP = N_Q_HEADS // N_KV_HEADS
WINDOW = 128
BLOCK = 128
CF_CONV = 31
N_BRANCHES = 4
EPS = 1e-6
NEG_INF = -1e30

OFF_A = 0
OFF_B = OFF_A + 2 * BRANCH_W
OFF_C = OFF_B + 3 * BRANCH_W
OFF_D = OFF_C + BRANCH_W + 2 * N_KV_HEADS * HEAD_DIM
OFF_G = OFF_D + 2 * BRANCH_W

SUBLANES = 8
LANES = 128
BF16_ROWS = 16
CONV_A_HALO = 8
CONV_D_HALO = 32
CONV_D_ROWS = 64
TS = 256
TM = 512
VMEM_LIMIT_MIXER = 56 * 1024 * 1024
VMEM_LIMIT_FFN = 56 * 1024 * 1024

ALIBI_SLOPES = tuple(2.0 ** (-8.0 * (i + 1) / N_Q_HEADS) for i in range(N_Q_HEADS))


def _rms(x, g):
    return x * lax.rsqrt(jnp.mean(x * x, axis=-1, keepdims=True) + EPS) * g


def _bdot(a, b):
    return jnp.dot(a.astype(jnp.bfloat16), b, preferred_element_type=jnp.float32)


def _causal_conv(buf_ref, w_ref, halo, width, rows):
    acc = None
    for k in range(width):
        term = w_ref[k:k + 1, :] * buf_ref[pl.ds(halo - (width - 1) + k, rows), :]
        acc = term if acc is None else acc + term
    return acc


def _causal_conv_wide(buf_ref, w_ref, out_ref, halo, width, rows):
    n_a = -(-width // SUBLANES)
    assert halo >= SUBLANES * n_a and rows % CONV_D_ROWS == 0
    ext = CONV_D_ROWS + SUBLANES
    for g in range(BRANCH_W // LANES):
        lanes = slice(g * LANES, (g + 1) * LANES)
        for t0 in range(0, rows, CONV_D_ROWS):
            slabs = [buf_ref[pl.ds(halo + t0 - SUBLANES * (a + 1), ext), lanes] for a in range(n_a)]
            y = None
            for r in range(SUBLANES):
                z = None
                for a in range(n_a):
                    k = width - 1 - (SUBLANES * a + r)
                    if k < 0:
                        continue
                    term = w_ref[k:k + 1, lanes] * slabs[a]
                    z = term if z is None else z + term
                shifted = z[SUBLANES - r:SUBLANES - r + CONV_D_ROWS, :]
                y = shifted if y is None else y + shifted
            out_ref[t0:t0 + CONV_D_ROWS, lanes] = y


def _mixer_kernel(sinks_ref, x_ref, n1g_ref, w_in_ref, caw_ref, cab_ref, wx_ref, bx_ref, wa_ref, ba_ref,
                  lam_ref, wao_ref, cbw_ref, wbo_ref, wco_ref, cdw_ref, cdb_ref, lng_ref, lnb_ref, wdo_ref,
                  wo_ref, o_ref, xn_s, ax_buf, cv_buf, d_buf, cd_buf, q_s, kv_buf, hc_ref):
    j = pl.program_id(1)
    ts = x_ref.shape[1]
    n_blk = ts // BLOCK
    bf16 = jnp.bfloat16
    f32 = jnp.float32
    cur_a = slice(CONV_A_HALO, CONV_A_HALO + ts)

    def proj(off, width):
        return jnp.dot(xn_s[...], w_in_ref[:, off:off + width], preferred_element_type=f32)

    def gate(idx):
        return jax.nn.sigmoid(proj(OFF_G + idx * D_MODEL, D_MODEL))

    def out_proj(act, w_ref):
        return jnp.dot(act, w_ref[...], preferred_element_type=f32)

    @pl.when(j == 0)
    def _():
        ax_buf[0:CONV_A_HALO, :] = jnp.zeros((CONV_A_HALO, BRANCH_W), f32)
        cv_buf[0:CONV_A_HALO, :] = jnp.zeros((CONV_A_HALO, BRANCH_W), f32)
        d_buf[0:CONV_D_HALO, :] = jnp.zeros((CONV_D_HALO, BRANCH_W), f32)
        kv_buf[:, 0:BLOCK, :] = jnp.zeros((6, BLOCK, LANES), bf16)
        hc_ref[...] = jnp.zeros_like(hc_ref)

    x = x_ref[0]
    xn_s[...] = _rms(x, n1g_ref[...]).astype(bf16)

    p_a = proj(OFF_A, 2 * BRANCH_W)
    ax_buf[cur_a, :] = p_a[:, :BRANCH_W]
    c = _causal_conv(ax_buf, caw_ref, CONV_A_HALO, LRU_CONV, ts) + cab_ref[...]
    ax_buf[0:CONV_A_HALO, :] = ax_buf[ts:ts + CONV_A_HALO, :]

    p_d = proj(OFF_D, 2 * BRANCH_W)
    d_buf[CONV_D_HALO:CONV_D_HALO + ts, :] = p_d[:, :BRANCH_W] * jax.nn.sigmoid(p_d[:, BRANCH_W:])

    gate_i = jax.nn.sigmoid(_bdot(c, wx_ref[...]) + bx_ref[...])
    gate_r = jax.nn.sigmoid(_bdot(c, wa_ref[...]) + ba_ref[...])
    z = -lam_ref[...]
    softplus = jnp.maximum(z, 0.0) + jnp.log1p(jnp.exp(-jnp.abs(z)))
    log_a = -LRU_C * gate_r * softplus
    a = jnp.exp(log_a)
    u = (c * gate_i) * jnp.sqrt(-jnp.tanh(log_a) * (a * a + 1.0))
    n_grp = ts // SUBLANES
    a3 = a.reshape(n_grp, SUBLANES, BRANCH_W)
    u3 = u.reshape(n_grp, SUBLANES, BRANCH_W)
    sub = lax.broadcasted_iota(jnp.int32, (n_grp, SUBLANES, BRANCH_W), 1)
    for step in (1, 2, 4):
        keep = sub >= step
        u3 = a3 * jnp.where(keep, pltpu.roll(u3, step, axis=1), 0.0) + u3
        a3 = a3 * jnp.where(keep, pltpu.roll(a3, step, axis=1), 1.0)
    carry = hc_ref[0:1, :]
    h_rows = []
    for i in range(n_grp):
        h_i = u3[i] + a3[i] * carry
        h_rows.append(h_i)
        carry = h_i[SUBLANES - 1:SUBLANES, :]
    hc_ref[0:1, :] = carry
    act_a = (jnp.concatenate(h_rows, axis=0) * jax.nn.gelu(p_a[:, BRANCH_W:])).astype(bf16)

    p_c = proj(OFF_C, BRANCH_W + 2 * N_KV_HEADS * HEAD_DIM)
    q_s[...] = (p_c[:, :BRANCH_W] * (HEAD_DIM ** -0.5)).astype(bf16)
    low = lax.broadcasted_iota(jnp.int32, (ts, LANES), 1) < HEAD_DIM
    k = p_c[:, BRANCH_W:BRANCH_W + LANES]
    v = p_c[:, BRANCH_W + LANES:BRANCH_W + 2 * LANES]
    k_sw = pltpu.roll(k, HEAD_DIM, axis=1)
    v_sw = pltpu.roll(v, HEAD_DIM, axis=1)
    zero = jnp.zeros_like(v)
    cur = slice(BLOCK, BLOCK + ts)
    kv_buf[0, cur, :] = jnp.where(low, k, k_sw).astype(bf16)
    kv_buf[1, cur, :] = jnp.where(low, k_sw, k).astype(bf16)
    kv_buf[2, cur, :] = jnp.where(low, v, zero).astype(bf16)
    kv_buf[3, cur, :] = jnp.where(low, zero, v_sw).astype(bf16)
    kv_buf[4, cur, :] = jnp.where(low, v_sw, zero).astype(bf16)
    kv_buf[5, cur, :] = jnp.where(low, zero, v).astype(bf16)
    low_q = lax.broadcasted_iota(jnp.int32, (BLOCK, LANES), 1) < HEAD_DIM
    scores = {}
    for blk in range(n_blk):
        keys = slice(blk * BLOCK, (blk + 2) * BLOCK)
        for pair in range(N_Q_HEADS // 2):
            kvh = (2 * pair) // GQA_GROUP
            q2 = q_s[blk * BLOCK:(blk + 1) * BLOCK, pair * LANES:(pair + 1) * LANES]
            zq = jnp.zeros_like(q2)
            qq = jnp.concatenate([jnp.where(low_q, q2, zq), jnp.where(low_q, zq, q2)], axis=0)
            scores[blk, pair] = lax.dot_general(qq, kv_buf[kvh, keys, :], (((1,), (1,)), ((), ())),
                                                preferred_element_type=f32)

    p_b = proj(OFF_B, 3 * BRANCH_W)
    cv_buf[cur_a, :] = p_b[:, BRANCH_W:2 * BRANCH_W] * p_b[:, :BRANCH_W]
    cb = _causal_conv(cv_buf, cbw_ref, CONV_A_HALO, SC_CONV, ts)
    cv_buf[0:CONV_A_HALO, :] = cv_buf[ts:ts + CONV_A_HALO, :]
    act_b = (p_b[:, 2 * BRANCH_W:] * cb).astype(bf16)
    gate_d = gate(3)
    gate_a = gate(0)

    _causal_conv_wide(d_buf, cdw_ref, cd_buf, CONV_D_HALO, CF_CONV, ts)
    d_buf[0:CONV_D_HALO, :] = d_buf[ts:ts + CONV_D_HALO, :]
    cd = cd_buf[...] + cdb_ref[...]
    mu = jnp.mean(cd, axis=-1, keepdims=True)
    var = jnp.mean(jnp.square(cd - mu), axis=-1, keepdims=True)
    ln = (cd - mu) * lax.rsqrt(var + EPS) * lng_ref[...] + lnb_ref[...]
    act_d = jax.nn.silu(ln).astype(bf16)
    merged = gate_d * out_proj(act_d, wdo_ref)
    gate_b = gate(1)

    qi = lax.broadcasted_iota(jnp.int32, (BLOCK, 2 * BLOCK), 0)
    ki = lax.broadcasted_iota(jnp.int32, (BLOCK, 2 * BLOCK), 1)
    dist = qi + BLOCK - ki
    in_window = (dist >= 0) & (dist < WINDOW)
    dist_f = dist.astype(f32)
    probs = {}
    for blk in range(n_blk):
        key_pos = ki + (j * ts + (blk - 1) * BLOCK)
        valid = in_window & (key_pos >= 0) if blk == 0 else in_window
        for pair in range(N_Q_HEADS // 2):
            both = []
            for half in range(2):
                head = 2 * pair + half
                s = scores[blk, pair][half * BLOCK:(half + 1) * BLOCK, :]
                s = jnp.where(valid, s - ALIBI_SLOPES[head] * dist_f, NEG_INF)
                sink = sinks_ref[head]
                m = jnp.maximum(jnp.max(s, axis=-1, keepdims=True), sink)
                e = jnp.exp(s - m)
                denom = jnp.sum(e, axis=-1, keepdims=True) + jnp.exp(sink - m)
                both.append((e / denom).astype(bf16))
            probs[blk, pair] = jnp.concatenate(both, axis=1)
    merged = merged + gate_a * out_proj(act_a, wao_ref)
    gate_c = gate(2)
    blocks_out = []
    for blk in range(n_blk):
        keys = slice(blk * BLOCK, (blk + 2) * BLOCK)
        pair_out = []
        for pair in range(N_Q_HEADS // 2):
            kvh = (2 * pair) // GQA_GROUP
            vv = jnp.concatenate([kv_buf[2 + 2 * kvh, keys, :], kv_buf[3 + 2 * kvh, keys, :]], axis=0)
            pair_out.append(jnp.dot(probs[blk, pair], vv, preferred_element_type=f32))
        blocks_out.append(jnp.concatenate(pair_out, axis=1))
    act_c = jnp.concatenate(blocks_out, axis=0).astype(bf16)
    kv_buf[:, 0:BLOCK, :] = kv_buf[:, ts:ts + BLOCK, :]

    merged = merged + gate_b * out_proj(act_b, wbo_ref)
    merged = merged + gate_c * out_proj(act_c, wco_ref)
    o_ref[0] = x + _bdot(merged, wo_ref[...])


def _ffn_kernel(x_ref, g_ref, wg_ref, wu_ref, wd_ref, fg_ref, o_ref, *, final_norm):
    x = x_ref[...]
    xn = _rms(x, g_ref[...]).astype(jnp.bfloat16)
    gte = jnp.dot(xn, wg_ref[...], preferred_element_type=jnp.float32)
    up = jnp.dot(xn, wu_ref[...], preferred_element_type=jnp.float32)
    y = x + _bdot(jax.nn.silu(gte) * up, wd_ref[...])
    if final_norm:
        y = _rms(y, fg_ref[...])
    o_ref[...] = y


def _with_casts(body, n_in, n_cast):
    def kern(*refs):
        ins, cast_in = refs[:n_in], refs[n_in:n_in + n_cast]
        out, cast_out = refs[n_in + n_cast], refs[n_in + n_cast + 1:n_in + 2 * n_cast + 1]
        body(*ins, out, *refs[n_in + 2 * n_cast + 1:])
        for src, dst in zip(cast_in, cast_out):
            dst[...] = src[...].astype(dst.dtype)
    return kern


def _layer_spec(arr, layer):
    rest = arr.shape[1:]
    return pl.BlockSpec((None,) + rest, lambda *_: (layer,) + (0,) * len(rest), pipeline_mode=pl.Buffered(1))


def _cast_plan(weights, layer, steps, step_of):
    views, in_specs, out_specs, out_shapes = [], [], [], []
    for w in weights:
        depth, r, c = w.shape
        rows = next(n for n in range(BF16_ROWS, r + 1, BF16_ROWS) if r % n == 0 and r // n <= steps)
        chunks = r // rows
        views.append(w.reshape(depth, chunks, rows, c))
        in_specs.append(pl.BlockSpec(
            (None, None, rows, c), lambda *g, n=chunks: (layer, jnp.minimum(step_of(*g), n - 1), 0, 0)))
        out_specs.append(pl.BlockSpec((None, rows, c), lambda *g, n=chunks: (jnp.minimum(step_of(*g), n - 1), 0, 0)))
        out_shapes.append(jax.ShapeDtypeStruct((chunks, rows, c), jnp.bfloat16))
    return views, in_specs, out_specs, out_shapes


def _mixer(x, params, cast_weights, cast_layer):
    b, s, d = x.shape
    n_j = s // TS
    x_spec = pl.BlockSpec((1, TS, d), lambda i, j: (i, j, 0))
    (sinks, sink_layer), params = params[0], params[1:]
    in_specs = [pl.BlockSpec(memory_space=pltpu.SMEM), x_spec] + [_layer_spec(p, l) for p, l in params]
    views, c_in, c_out, c_shapes = _cast_plan(cast_weights, cast_layer, b * n_j, lambda i, j: i * n_j + j)
    f32, bf16 = jnp.float32, jnp.bfloat16
    scratch = [
        pltpu.VMEM((TS, d), bf16),
        pltpu.VMEM((CONV_A_HALO + TS, BRANCH_W), f32),
        pltpu.VMEM((CONV_A_HALO + TS, BRANCH_W), f32),
        pltpu.VMEM((CONV_D_HALO + TS, BRANCH_W), f32),
        pltpu.VMEM((TS, BRANCH_W), f32),
        pltpu.VMEM((TS, BRANCH_W), bf16),
        pltpu.VMEM((6, BLOCK + TS, LANES), bf16),
        pltpu.VMEM((SUBLANES, BRANCH_W), f32),
    ]
    out, *cast = pl.pallas_call(
        _with_casts(_mixer_kernel, len(in_specs), len(views)),
        grid=(b, n_j),
        in_specs=in_specs + c_in,
        out_specs=[x_spec] + c_out,
        out_shape=[jax.ShapeDtypeStruct(x.shape, x.dtype)] + c_shapes,
        scratch_shapes=scratch,
        compiler_params=pltpu.CompilerParams(
            dimension_semantics=("arbitrary", "arbitrary"), vmem_limit_bytes=VMEM_LIMIT_MIXER),
        name="mixer",
    )(sinks[sink_layer], x, *[p for p, _ in params], *views)
    return out, [c.reshape(1, -1, c.shape[-1]) for c in cast]


def _ffn(x2, params, final_g, final_norm, cast_weights, cast_layer):
    t, d = x2.shape
    x_spec = pl.BlockSpec((TM, d), lambda i: (i, 0))
    fg_spec = pl.BlockSpec(final_g.shape, lambda i: (0, 0), pipeline_mode=pl.Buffered(1))
    in_specs = [x_spec] + [_layer_spec(p, l) for p, l in params] + [fg_spec]
    views, c_in, c_out, c_shapes = _cast_plan(cast_weights, cast_layer, t // TM, lambda i: i)
    out, *cast = pl.pallas_call(
        _with_casts(functools.partial(_ffn_kernel, final_norm=final_norm), len(in_specs), len(views)),
        grid=(t // TM,),
        in_specs=in_specs + c_in,
        out_specs=[x_spec] + c_out,
        out_shape=[jax.ShapeDtypeStruct(x2.shape, x2.dtype)] + c_shapes,
        compiler_params=pltpu.CompilerParams(
            dimension_semantics=("arbitrary",), vmem_limit_bytes=VMEM_LIMIT_FFN),
        name="ffn",
    )(x2, *[p for p, _ in params], final_g, *views)
    return out, [c.reshape(1, -1, c.shape[-1]) for c in cast]


def _block_diag(w):
    l, h, wb, _ = w.shape
    eye = jnp.eye(h, dtype=w.dtype)
    return (w[:, :, :, None, :] * eye[None, :, None, :, None]).reshape(l, h * wb, h * wb)


def kernel(x, norm1_g, w_in, conv_a_w, conv_a_b, lru_wx, lru_bx, lru_wa, lru_ba, lru_lambda, w_a_out, conv_b_w, w_b_out, sinks, w_c_out, conv_d_w, conv_d_b, ln_d_g, ln_d_b, w_d_out, w_o, norm2_g, w_ffn_gate, w_ffn_up, w_ffn_down, final_g):
    b, s, d = x.shape
    depth = w_in.shape[0]
    bf16 = jnp.bfloat16
    rows = lambda a: a.reshape(depth, 1, -1)
    wx_bd = _block_diag(lru_wx).astype(bf16)
    wa_bd = _block_diag(lru_wa).astype(bf16)
    mixer_w = (w_in, w_a_out, w_b_out, w_c_out, w_d_out, w_o)
    ffn_w = (w_ffn_gate, w_ffn_up, w_ffn_down)
    win, wao, wbo, wco, wdo, wo = [(w[0:1].astype(bf16), 0) for w in mixer_w]
    for l in range(depth):
        mixer_params = [
            (sinks, l), (rows(norm1_g), l), win, (conv_a_w, l), (rows(conv_a_b), l), (wx_bd, l), (rows(lru_bx), l),
            (wa_bd, l), (rows(lru_ba), l), (rows(lru_lambda), l), wao, (conv_b_w, l), wbo, wco, (conv_d_w, l),
            (rows(conv_d_b), l), (rows(ln_d_g), l), (rows(ln_d_b), l), wdo, wo,
        ]
        x, ffn_bf = _mixer(x, mixer_params, ffn_w, l)
        ffn_params = [(rows(norm2_g), l)] + [(w, 0) for w in ffn_bf]
        last = l == depth - 1
        x, nxt = _ffn(x.reshape(b * s, d), ffn_params, final_g.reshape(1, -1), last, () if last else mixer_w, l + 1)
        x = x.reshape(b, s, d)
        if not last:
            win, wao, wbo, wco, wdo, wo = [(w, 0) for w in nxt]
    return x
```

```python
import functools

import jax
import jax.numpy as jnp
from jax import lax
from jax.experimental import pallas as pl
from jax.experimental.pallas import tpu as pltpu

D_MODEL = 1024
BRANCH_W = 512
LRU_CONV = 4
LRU_C = 8.0
SC_CONV = 3
HEAD_DIM = 64
N_Q_HEADS = 8
N_KV_HEADS = 2
GQA_GROUP = N_Q_HEADS // N_KV_HEADS
WINDOW = 128
BLOCK = 128
CF_CONV = 31
N_BRANCHES = 4
EPS = 1e-6
NEG_INF = -1e30

OFF_A = 0
OFF_B = OFF_A + 2 * BRANCH_W
OFF_C = OFF_B + 3 * BRANCH_W
OFF_D = OFF_C + BRANCH_W + 2 * N_KV_HEADS * HEAD_DIM
OFF_G = OFF_D + 2 * BRANCH_W

SUBLANES = 8
LANES = 128
BF16_ROWS = 16
MXU_N = 256
CONV_A_HALO = 8
CONV_D_HALO = 32
CONV_D_ROWS = 64
TS = 256
TM = 512
VMEM_LIMIT_MIXER = 56 * 1024 * 1024
VMEM_LIMIT_FFN = 56 * 1024 * 1024

VEC_LAYOUT = {}
for _name, _width in (("norm1_g", D_MODEL), ("conv_a_b", BRANCH_W), ("lru_bx", BRANCH_W), ("lru_ba", BRANCH_W),
                      ("lru_lambda", BRANCH_W), ("conv_d_b", BRANCH_W), ("ln_d_g", BRANCH_W), ("ln_d_b", BRANCH_W)):
    VEC_LAYOUT[_name] = (sum(w for _, w in VEC_LAYOUT.values()), _width)

ALIBI_SLOPES = tuple(2.0 ** (-8.0 * (i + 1) / N_Q_HEADS) for i in range(N_Q_HEADS))


def _rms(x, g):
    return x * lax.rsqrt(jnp.mean(x * x, axis=-1, keepdims=True) + EPS) * g


def _bdot(a, b):
    return jnp.dot(a.astype(jnp.bfloat16), b, preferred_element_type=jnp.float32)


def _causal_conv(buf_ref, w_ref, halo, width, rows):
    acc = None
    for k in range(width):
        term = w_ref[k:k + 1, :] * buf_ref[pl.ds(halo - (width - 1) + k, rows), :]
        acc = term if acc is None else acc + term
    return acc


def _causal_conv_wide(buf_ref, w_ref, out_ref, halo, width, rows):
    n_a = -(-width // SUBLANES)
    assert halo >= SUBLANES * n_a and rows % CONV_D_ROWS == 0
    ext = CONV_D_ROWS + SUBLANES
    for g in range(BRANCH_W // LANES):
        lanes = slice(g * LANES, (g + 1) * LANES)
        for t0 in range(0, rows, CONV_D_ROWS):
            slabs = [buf_ref[pl.ds(halo + t0 - SUBLANES * (a + 1), ext), lanes] for a in range(n_a)]
            y = None
            for r in range(SUBLANES):
                z = None
                for a in range(n_a):
                    k = width - 1 - (SUBLANES * a + r)
                    if k < 0:
                        continue
                    term = w_ref[k:k + 1, lanes] * slabs[a]
                    z = term if z is None else z + term
                shifted = z[SUBLANES - r:SUBLANES - r + CONV_D_ROWS, :]
                y = shifted if y is None else y + shifted
            out_ref[t0:t0 + CONV_D_ROWS, lanes] = y


def _mixer_kernel(sinks_ref, x_ref, vec_ref, w_in_ref, caw_ref, wx_ref, wa_ref, wao_ref, cbw_ref, wbo_ref, wco_ref,
                  cdw_ref, wdo_ref, wo_ref, o_ref, xn_s, ax_buf, cv_buf, d_buf, cd_buf, q_s, kv_buf, hc_ref):
    j = pl.program_id(1)
    ts = x_ref.shape[1]
    n_blk = ts // BLOCK
    bf16 = jnp.bfloat16
    f32 = jnp.float32
    cur_a = slice(CONV_A_HALO, CONV_A_HALO + ts)

    def vec(name):
        off, width = VEC_LAYOUT[name]
        return vec_ref[:, off:off + width]

    n1g, cab, bx, ba, lam = vec("norm1_g"), vec("conv_a_b"), vec("lru_bx"), vec("lru_ba"), vec("lru_lambda")
    cdb, lng, lnb = vec("conv_d_b"), vec("ln_d_g"), vec("ln_d_b")

    def proj(off, width):
        return jnp.dot(xn_s[...], w_in_ref[:, off:off + width], preferred_element_type=f32)

    def gate(idx):
        return jax.nn.sigmoid(proj(OFF_G + idx * D_MODEL, D_MODEL))

    def out_proj(act, w_ref):
        return jnp.dot(act, w_ref[...], preferred_element_type=f32)

    @pl.when(j == 0)
    def _():
        ax_buf[0:CONV_A_HALO, :] = jnp.zeros((CONV_A_HALO, BRANCH_W), f32)
        cv_buf[0:CONV_A_HALO, :] = jnp.zeros((CONV_A_HALO, BRANCH_W), f32)
        d_buf[0:CONV_D_HALO, :] = jnp.zeros((CONV_D_HALO, BRANCH_W), f32)
        kv_buf[:, 0:BLOCK, :] = jnp.zeros((6, BLOCK, LANES), bf16)
        hc_ref[...] = jnp.zeros_like(hc_ref)

    x = x_ref[0]
    xn_s[...] = _rms(x, n1g).astype(bf16)

    p_a = proj(OFF_A, 2 * BRANCH_W)
    ax_buf[cur_a, :] = p_a[:, :BRANCH_W]
    c = _causal_conv(ax_buf, caw_ref, CONV_A_HALO, LRU_CONV, ts) + cab
    ax_buf[0:CONV_A_HALO, :] = ax_buf[ts:ts + CONV_A_HALO, :]

    p_d = proj(OFF_D, 2 * BRANCH_W)
    d_buf[CONV_D_HALO:CONV_D_HALO + ts, :] = p_d[:, :BRANCH_W] * jax.nn.sigmoid(p_d[:, BRANCH_W:])

    c_bf = c.astype(bf16)

    def lru_gate(w_ref, bias):
        parts = [jnp.dot(c_bf[:, i * MXU_N:(i + 1) * MXU_N], w_ref[i], preferred_element_type=f32)
                 for i in range(BRANCH_W // MXU_N)]
        return jax.nn.sigmoid(jnp.concatenate(parts, axis=1) + bias)

    gate_i = lru_gate(wx_ref, bx)
    gate_r = lru_gate(wa_ref, ba)
    z = -lam
    softplus = jnp.maximum(z, 0.0) + jnp.log1p(jnp.exp(-jnp.abs(z)))
    log_a = -LRU_C * gate_r * softplus
    a = jnp.exp(log_a)
    u = (c * gate_i) * jnp.sqrt(-jnp.tanh(log_a) * (a * a + 1.0))
    n_grp = ts // SUBLANES
    a3 = a.reshape(n_grp, SUBLANES, BRANCH_W)
    u3 = u.reshape(n_grp, SUBLANES, BRANCH_W)
    sub = lax.broadcasted_iota(jnp.int32, (n_grp, SUBLANES, BRANCH_W), 1)
    for step in (1, 2, 4):
        keep = sub >= step
        u3 = a3 * jnp.where(keep, pltpu.roll(u3, step, axis=1), 0.0) + u3
        a3 = a3 * jnp.where(keep, pltpu.roll(a3, step, axis=1), 1.0)
    carry = hc_ref[0:1, :]
    h_rows = []
    for i in range(n_grp):
        h_i = u3[i] + a3[i] * carry
        h_rows.append(h_i)
        carry = h_i[SUBLANES - 1:SUBLANES, :]
    hc_ref[0:1, :] = carry
    act_a = (jnp.concatenate(h_rows, axis=0) * jax.nn.gelu(p_a[:, BRANCH_W:])).astype(bf16)

    p_c = proj(OFF_C, BRANCH_W + 2 * N_KV_HEADS * HEAD_DIM)
    q_s[...] = (p_c[:, :BRANCH_W] * (HEAD_DIM ** -0.5)).astype(bf16)
    low = lax.broadcasted_iota(jnp.int32, (ts, LANES), 1) < HEAD_DIM
    k = p_c[:, BRANCH_W:BRANCH_W + LANES]
    v = p_c[:, BRANCH_W + LANES:BRANCH_W + 2 * LANES]
    k_sw = pltpu.roll(k, HEAD_DIM, axis=1)
    v_sw = pltpu.roll(v, HEAD_DIM, axis=1)
    zero = jnp.zeros_like(v)
    cur = slice(BLOCK, BLOCK + ts)
    kv_buf[0, cur, :] = jnp.where(low, k, k_sw).astype(bf16)
    kv_buf[1, cur, :] = jnp.where(low, k_sw, k).astype(bf16)
    kv_buf[2, cur, :] = jnp.where(low, v, zero).astype(bf16)
    kv_buf[3, cur, :] = jnp.where(low, zero, v_sw).astype(bf16)
    kv_buf[4, cur, :] = jnp.where(low, v_sw, zero).astype(bf16)
    kv_buf[5, cur, :] = jnp.where(low, zero, v).astype(bf16)
    low_q = lax.broadcasted_iota(jnp.int32, (BLOCK, LANES), 1) < HEAD_DIM
    scores = {}
    for blk in range(n_blk):
        keys = slice(blk * BLOCK, (blk + 2) * BLOCK)
        for pair in range(N_Q_HEADS // 2):
            kvh = (2 * pair) // GQA_GROUP
            q2 = q_s[blk * BLOCK:(blk + 1) * BLOCK, pair * LANES:(pair + 1) * LANES]
            zq = jnp.zeros_like(q2)
            qq = jnp.concatenate([jnp.where(low_q, q2, zq), jnp.where(low_q, zq, q2)], axis=0)
            scores[blk, pair] = lax.dot_general(qq, kv_buf[kvh, keys, :], (((1,), (1,)), ((), ())),
                                                preferred_element_type=f32)

    p_b = proj(OFF_B, 3 * BRANCH_W)
    cv_buf[cur_a, :] = p_b[:, BRANCH_W:2 * BRANCH_W] * p_b[:, :BRANCH_W]
    cb = _causal_conv(cv_buf, cbw_ref, CONV_A_HALO, SC_CONV, ts)
    cv_buf[0:CONV_A_HALO, :] = cv_buf[ts:ts + CONV_A_HALO, :]
    act_b = (p_b[:, 2 * BRANCH_W:] * cb).astype(bf16)
    gate_d = gate(3)
    gate_a = gate(0)

    _causal_conv_wide(d_buf, cdw_ref, cd_buf, CONV_D_HALO, CF_CONV, ts)
    d_buf[0:CONV_D_HALO, :] = d_buf[ts:ts + CONV_D_HALO, :]
    cd = cd_buf[...] + cdb
    mu = jnp.mean(cd, axis=-1, keepdims=True)
    var = jnp.mean(jnp.square(cd - mu), axis=-1, keepdims=True)
    ln = (cd - mu) * lax.rsqrt(var + EPS) * lng + lnb
    act_d = jax.nn.silu(ln).astype(bf16)
    merged = gate_d * out_proj(act_d, wdo_ref)
    gate_b = gate(1)

    assert WINDOW == BLOCK
    qi = lax.broadcasted_iota(jnp.int32, (BLOCK, BLOCK), 0)
    ki = lax.broadcasted_iota(jnp.int32, (BLOCK, BLOCK), 1)
    from_prev = ki > qi
    dist_f = jnp.where(from_prev, qi + BLOCK - ki, qi - ki).astype(f32)
    no_prev = from_prev & (j == 0)
    probs = {}
    for blk in range(n_blk):
        for pair in range(N_Q_HEADS // 2):
            both = []
            for half in range(2):
                head = 2 * pair + half
                s2 = scores[blk, pair][half * BLOCK:(half + 1) * BLOCK, :]
                s = jnp.where(from_prev, s2[:, :BLOCK], s2[:, BLOCK:]) - ALIBI_SLOPES[head] * dist_f
                if blk == 0:
                    s = jnp.where(no_prev, NEG_INF, s)
                sink = sinks_ref[head]
                m = jnp.maximum(jnp.max(s, axis=-1, keepdims=True), sink)
                e = jnp.exp(s - m)
                denom = jnp.sum(e, axis=-1, keepdims=True) + jnp.exp(sink - m)
                p = e / denom
                both += [jnp.where(from_prev, p, 0.0).astype(bf16), jnp.where(from_prev, 0.0, p).astype(bf16)]
            probs[blk, pair] = jnp.concatenate(both, axis=1)
    merged = merged + gate_a * out_proj(act_a, wao_ref)
    gate_c = gate(2)
    blocks_out = []
    for blk in range(n_blk):
        keys = slice(blk * BLOCK, (blk + 2) * BLOCK)
        pair_out = []
        for pair in range(N_Q_HEADS // 2):
            kvh = (2 * pair) // GQA_GROUP
            vv = jnp.concatenate([kv_buf[2 + 2 * kvh, keys, :], kv_buf[3 + 2 * kvh, keys, :]], axis=0)
            pair_out.append(jnp.dot(probs[blk, pair], vv, preferred_element_type=f32))
        blocks_out.append(jnp.concatenate(pair_out, axis=1))
    act_c = jnp.concatenate(blocks_out, axis=0).astype(bf16)
    kv_buf[:, 0:BLOCK, :] = kv_buf[:, ts:ts + BLOCK, :]

    merged = merged + gate_b * out_proj(act_b, wbo_ref)
    merged = merged + gate_c * out_proj(act_c, wco_ref)
    o_ref[0] = x + _bdot(merged, wo_ref[...])


def _ffn_kernel(x_ref, g_ref, wg_ref, wu_ref, wd_ref, fg_ref, o_ref, *, final_norm):
    x = x_ref[...]
    xn = _rms(x, g_ref[...]).astype(jnp.bfloat16)
    gte = jnp.dot(xn, wg_ref[...], preferred_element_type=jnp.float32)
    up = jnp.dot(xn, wu_ref[...], preferred_element_type=jnp.float32)
    y = x + _bdot(jax.nn.silu(gte) * up, wd_ref[...])
    if final_norm:
        y = _rms(y, fg_ref[...])
    o_ref[...] = y


def _with_casts(body, n_in, n_cast):
    def kern(*refs):
        ins, cast_in = refs[:n_in], refs[n_in:n_in + n_cast]
        out, cast_out = refs[n_in + n_cast], refs[n_in + n_cast + 1:n_in + 2 * n_cast + 1]
        body(*ins, out, *refs[n_in + 2 * n_cast + 1:])
        for src, dst in zip(cast_in, cast_out):
            dst[...] = src[...].astype(dst.dtype)
    return kern


def _layer_spec(arr, layer):
    rest = arr.shape[1:]
    return pl.BlockSpec((None,) + rest, lambda *_: (layer,) + (0,) * len(rest), pipeline_mode=pl.Buffered(1))


def _cast_plan(weights, layer, steps, step_of):
    views, in_specs, out_specs, out_shapes = [], [], [], []
    for w in weights:
        depth, r, c = w.shape
        rows = next(n for n in range(BF16_ROWS, r + 1, BF16_ROWS) if r % n == 0 and r // n <= steps)
        chunks = r // rows
        views.append(w.reshape(depth, chunks, rows, c))
        in_specs.append(pl.BlockSpec(
            (None, None, rows, c), lambda *g, n=chunks: (layer, jnp.minimum(step_of(*g), n - 1), 0, 0)))
        out_specs.append(pl.BlockSpec((None, rows, c), lambda *g, n=chunks: (jnp.minimum(step_of(*g), n - 1), 0, 0)))
        out_shapes.append(jax.ShapeDtypeStruct((chunks, rows, c), jnp.bfloat16))
    return views, in_specs, out_specs, out_shapes


def _mixer(x, params, cast_weights, cast_layer):
    b, s, d = x.shape
    n_j = s // TS
    x_spec = pl.BlockSpec((1, TS, d), lambda i, j: (i, j, 0))
    (sinks, sink_layer), params = params[0], params[1:]
    in_specs = [pl.BlockSpec(memory_space=pltpu.SMEM), x_spec] + [_layer_spec(p, l) for p, l in params]
    views, c_in, c_out, c_shapes = _cast_plan(cast_weights, cast_layer, b * n_j, lambda i, j: i * n_j + j)
    f32, bf16 = jnp.float32, jnp.bfloat16
    scratch = [
        pltpu.VMEM((TS, d), bf16),
        pltpu.VMEM((CONV_A_HALO + TS, BRANCH_W), f32),
        pltpu.VMEM((CONV_A_HALO + TS, BRANCH_W), f32),
        pltpu.VMEM((CONV_D_HALO + TS, BRANCH_W), f32),
        pltpu.VMEM((TS, BRANCH_W), f32),
        pltpu.VMEM((TS, BRANCH_W), bf16),
        pltpu.VMEM((6, BLOCK + TS, LANES), bf16),
        pltpu.VMEM((SUBLANES, BRANCH_W), f32),
    ]
    out, *cast = pl.pallas_call(
        _with_casts(_mixer_kernel, len(in_specs), len(views)),
        grid=(b, n_j),
        in_specs=in_specs + c_in,
        out_specs=[x_spec] + c_out,
        out_shape=[jax.ShapeDtypeStruct(x.shape, x.dtype)] + c_shapes,
        scratch_shapes=scratch,
        compiler_params=pltpu.CompilerParams(
            dimension_semantics=("arbitrary", "arbitrary"), vmem_limit_bytes=VMEM_LIMIT_MIXER),
        name="mixer",
    )(sinks[sink_layer], x, *[p for p, _ in params], *views)
    return out, [c.reshape(1, -1, c.shape[-1]) for c in cast]


def _ffn(x2, params, final_g, final_norm, cast_weights, cast_layer):
    t, d = x2.shape
    x_spec = pl.BlockSpec((TM, d), lambda i: (i, 0))
    fg_spec = pl.BlockSpec(final_g.shape, lambda i: (0, 0), pipeline_mode=pl.Buffered(1))
    in_specs = [x_spec] + [_layer_spec(p, l) for p, l in params] + [fg_spec]
    views, c_in, c_out, c_shapes = _cast_plan(cast_weights, cast_layer, t // TM, lambda i: i)
    out, *cast = pl.pallas_call(
        _with_casts(functools.partial(_ffn_kernel, final_norm=final_norm), len(in_specs), len(views)),
        grid=(t // TM,),
        in_specs=in_specs + c_in,
        out_specs=[x_spec] + c_out,
        out_shape=[jax.ShapeDtypeStruct(x2.shape, x2.dtype)] + c_shapes,
        compiler_params=pltpu.CompilerParams(
            dimension_semantics=("arbitrary",), vmem_limit_bytes=VMEM_LIMIT_FFN),
        name="ffn",
    )(x2, *[p for p, _ in params], final_g, *views)
    return out, [c.reshape(1, -1, c.shape[-1]) for c in cast]


def _block_diag(w):
    l, h, wb, _ = w.shape
    per = MXU_N // wb
    eye = jnp.eye(per, dtype=w.dtype)
    w = w.reshape(l, h // per, per, wb, wb)
    return (w[:, :, :, :, None, :] * eye[None, None, :, None, :, None]).reshape(l, h // per, MXU_N, MXU_N)


def kernel(x, norm1_g, w_in, conv_a_w, conv_a_b, lru_wx, lru_bx, lru_wa, lru_ba, lru_lambda, w_a_out, conv_b_w, w_b_out, sinks, w_c_out, conv_d_w, conv_d_b, ln_d_g, ln_d_b, w_d_out, w_o, norm2_g, w_ffn_gate, w_ffn_up, w_ffn_down, final_g):
    b, s, d = x.shape
    depth = w_in.shape[0]
    bf16 = jnp.bfloat16
    rows = lambda a: a.reshape(depth, 1, -1)
    wx_bd = _block_diag(lru_wx).astype(bf16)
    wa_bd = _block_diag(lru_wa).astype(bf16)
    mixer_w = (w_in, w_a_out, w_b_out, w_c_out, w_d_out, w_o)
    ffn_w = (w_ffn_gate, w_ffn_up, w_ffn_down)
    vecs = dict(norm1_g=norm1_g, conv_a_b=conv_a_b, lru_bx=lru_bx, lru_ba=lru_ba, lru_lambda=lru_lambda,
                conv_d_b=conv_d_b, ln_d_g=ln_d_g, ln_d_b=ln_d_b)
    vec = jnp.concatenate([vecs[name] for name in VEC_LAYOUT], axis=-1)[:, None, :]
    win, wao, wbo, wco, wdo, wo = [(w[0:1].astype(bf16), 0) for w in mixer_w]
    for l in range(depth):
        mixer_params = [
            (sinks, l), (vec, l), win, (conv_a_w, l), (wx_bd, l), (wa_bd, l), wao, (conv_b_w, l), wbo, wco,
            (conv_d_w, l), wdo, wo,
        ]
        x, ffn_bf = _mixer(x, mixer_params, ffn_w, l)
        ffn_params = [(rows(norm2_g), l)] + [(w, 0) for w in ffn_bf]
        last = l == depth - 1
        x, nxt = _ffn(x.reshape(b * s, d), ffn_params, final_g.reshape(1, -1), last, () if last else mixer_w, l + 1)
        x = x.reshape(b, s, d)
        if not last:
            win, wao, wbo, wco, wdo, wo = [(w, 0) for w in nxt]
    return x
```

```python
import functools

import jax
import jax.numpy as jnp
from jax import lax
from jax.experimental import pallas as pl
from jax.experimental.pallas import tpu as pltpu

D_MODEL = 1024
BRANCH_W = 512
LRU_CONV = 4
LRU_C = 8.0
SC_CONV = 3
HEAD_DIM = 64
N_Q_HEADS = 8
N_KV_HEADS = 2
GQA_GROUP = N_Q_HEADS // N_KV_HEADS
WINDOW = 128
BLOCK = 128
CF_CONV = 31
N_BRANCHES = 4
EPS = 1e-6
NEG_INF = -1e30

OFF_A = 0
OFF_B = OFF_A + 2 * BRANCH_W
OFF_C = OFF_B + 3 * BRANCH_W
OFF_D = OFF_C + BRANCH_W + 2 * N_KV_HEADS * HEAD_DIM
OFF_G = OFF_D + 2 * BRANCH_W

SUBLANES = 8
LANES = 128
BF16_ROWS = 16
MXU_N = 256
CONV_A_HALO = 8
CONV_D_HALO = 32
CONV_D_ROWS = 64
TS = 256
TM = 512
VMEM_LIMIT_MIXER = 56 * 1024 * 1024
VMEM_LIMIT_FFN = 56 * 1024 * 1024

VEC_LAYOUT = {}
for _name, _width in (("norm1_g", D_MODEL), ("conv_a_b", BRANCH_W), ("lru_bx", BRANCH_W), ("lru_ba", BRANCH_W),
                      ("lru_lambda", BRANCH_W), ("conv_d_b", BRANCH_W), ("ln_d_g", BRANCH_W), ("ln_d_b", BRANCH_W)):
    VEC_LAYOUT[_name] = (sum(w for _, w in VEC_LAYOUT.values()), _width)

ALIBI_SLOPES = tuple(2.0 ** (-8.0 * (i + 1) / N_Q_HEADS) for i in range(N_Q_HEADS))


def _rms(x, g):
    return x * lax.rsqrt(jnp.mean(x * x, axis=-1, keepdims=True) + EPS) * g


def _bdot(a, b):
    return jnp.dot(a.astype(jnp.bfloat16), b, preferred_element_type=jnp.float32)


def _causal_conv(buf_ref, w_ref, halo, width, rows):
    acc = None
    for k in range(width):
        term = w_ref[k:k + 1, :] * buf_ref[pl.ds(halo - (width - 1) + k, rows), :]
        acc = term if acc is None else acc + term
    return acc


def _causal_conv_wide(buf_ref, w_ref, out_ref, halo, width, rows):
    n_a = -(-width // SUBLANES)
    assert halo >= SUBLANES * n_a and rows % CONV_D_ROWS == 0
    ext = CONV_D_ROWS + SUBLANES
    for g in range(BRANCH_W // LANES):
        lanes = slice(g * LANES, (g + 1) * LANES)
        for t0 in range(0, rows, CONV_D_ROWS):
            slabs = [buf_ref[pl.ds(halo + t0 - SUBLANES * (a + 1), ext), lanes] for a in range(n_a)]
            y = None
            for r in range(SUBLANES):
                z = None
                for a in range(n_a):
                    k = width - 1 - (SUBLANES * a + r)
                    if k < 0:
                        continue
                    term = w_ref[k:k + 1, lanes] * slabs[a]
                    z = term if z is None else z + term
                shifted = z[SUBLANES - r:SUBLANES - r + CONV_D_ROWS, :]
                y = shifted if y is None else y + shifted
            out_ref[t0:t0 + CONV_D_ROWS, lanes] = y


def _mixer_kernel(sinks_ref, x_ref, vec_ref, w_in_ref, caw_ref, wx_ref, wa_ref, wao_ref, cbw_ref, wbo_ref, wco_ref,
                  cdw_ref, wdo_ref, wo_ref, o_ref, xn_s, ax_buf, cv_buf, d_buf, cd_buf, q_s, kv_buf, hc_ref):
    j = pl.program_id(1)
    ts = x_ref.shape[1]
    n_blk = ts // BLOCK
    bf16 = jnp.bfloat16
    f32 = jnp.float32
    cur_a = slice(CONV_A_HALO, CONV_A_HALO + ts)

    def vec(name):
        off, width = VEC_LAYOUT[name]
        return vec_ref[:, off:off + width]

    n1g, cab, bx, ba, lam = vec("norm1_g"), vec("conv_a_b"), vec("lru_bx"), vec("lru_ba"), vec("lru_lambda")
    cdb, lng, lnb = vec("conv_d_b"), vec("ln_d_g"), vec("ln_d_b")

    def proj(off, width):
        return jnp.dot(xn_s[...], w_in_ref[:, off:off + width], preferred_element_type=f32)

    def gate(idx):
        return jax.nn.sigmoid(proj(OFF_G + idx * D_MODEL, D_MODEL))

    def out_proj(act, w_ref):
        return jnp.dot(act, w_ref[...], preferred_element_type=f32)

    @pl.when(j == 0)
    def _():
        ax_buf[0:CONV_A_HALO, :] = jnp.zeros((CONV_A_HALO, BRANCH_W), f32)
        cv_buf[0:CONV_A_HALO, :] = jnp.zeros((CONV_A_HALO, BRANCH_W), f32)
        d_buf[0:CONV_D_HALO, :] = jnp.zeros((CONV_D_HALO, BRANCH_W), f32)
        kv_buf[:, 0:BLOCK, :] = jnp.zeros((6, BLOCK, LANES), bf16)
        hc_ref[...] = jnp.zeros_like(hc_ref)

    x = x_ref[0]
    xn_s[...] = _rms(x, n1g).astype(bf16)

    p_a = proj(OFF_A, 2 * BRANCH_W)
    ax_buf[cur_a, :] = p_a[:, :BRANCH_W]
    c = _causal_conv(ax_buf, caw_ref, CONV_A_HALO, LRU_CONV, ts) + cab
    ax_buf[0:CONV_A_HALO, :] = ax_buf[ts:ts + CONV_A_HALO, :]

    p_d = proj(OFF_D, 2 * BRANCH_W)
    d_buf[CONV_D_HALO:CONV_D_HALO + ts, :] = p_d[:, :BRANCH_W] * jax.nn.sigmoid(p_d[:, BRANCH_W:])

    c_bf = c.astype(bf16)

    def lru_gate(w_ref, bias):
        parts = [jnp.dot(c_bf[:, i * MXU_N:(i + 1) * MXU_N], w_ref[i], preferred_element_type=f32)
                 for i in range(BRANCH_W // MXU_N)]
        return jax.nn.sigmoid(jnp.concatenate(parts, axis=1) + bias)

    gate_i = lru_gate(wx_ref, bx)
    gate_r = lru_gate(wa_ref, ba)
    z = -lam
    softplus = jnp.maximum(z, 0.0) + jnp.log1p(jnp.exp(-jnp.abs(z)))
    log_a = -LRU_C * gate_r * softplus
    a = jnp.exp(log_a)
    u = (c * gate_i) * jnp.sqrt(-jnp.tanh(log_a) * (a * a + 1.0))
    n_grp = ts // SUBLANES
    a3 = a.reshape(n_grp, SUBLANES, BRANCH_W)
    u3 = u.reshape(n_grp, SUBLANES, BRANCH_W)
    sub = lax.broadcasted_iota(jnp.int32, (n_grp, SUBLANES, BRANCH_W), 1)
    for step in (1, 2, 4):
        keep = sub >= step
        u3 = a3 * jnp.where(keep, pltpu.roll(u3, step, axis=1), 0.0) + u3
        a3 = a3 * jnp.where(keep, pltpu.roll(a3, step, axis=1), 1.0)
    carry = hc_ref[0:1, :]
    h_rows = []
    for i in range(n_grp):
        h_i = u3[i] + a3[i] * carry
        h_rows.append(h_i)
        carry = h_i[SUBLANES - 1:SUBLANES, :]
    hc_ref[0:1, :] = carry
    act_a = (jnp.concatenate(h_rows, axis=0) * jax.nn.gelu(p_a[:, BRANCH_W:])).astype(bf16)

    p_c = proj(OFF_C, BRANCH_W + 2 * N_KV_HEADS * HEAD_DIM)
    q_s[...] = (p_c[:, :BRANCH_W] * (HEAD_DIM ** -0.5)).astype(bf16)
    low = lax.broadcasted_iota(jnp.int32, (ts, LANES), 1) < HEAD_DIM
    k = p_c[:, BRANCH_W:BRANCH_W + LANES]
    v = p_c[:, BRANCH_W + LANES:BRANCH_W + 2 * LANES]
    k_sw = pltpu.roll(k, HEAD_DIM, axis=1)
    v_sw = pltpu.roll(v, HEAD_DIM, axis=1)
    zero = jnp.zeros_like(v)
    cur = slice(BLOCK, BLOCK + ts)
    kv_buf[0, cur, :] = jnp.where(low, k, k_sw).astype(bf16)
    kv_buf[1, cur, :] = jnp.where(low, k_sw, k).astype(bf16)
    kv_buf[2, cur, :] = jnp.where(low, v, zero).astype(bf16)
    kv_buf[3, cur, :] = jnp.where(low, zero, v_sw).astype(bf16)
    kv_buf[4, cur, :] = jnp.where(low, v_sw, zero).astype(bf16)
    kv_buf[5, cur, :] = jnp.where(low, zero, v).astype(bf16)

    p_b = proj(OFF_B, 3 * BRANCH_W)
    cv_buf[cur_a, :] = p_b[:, BRANCH_W:2 * BRANCH_W] * p_b[:, :BRANCH_W]
    cb = _causal_conv(cv_buf, cbw_ref, CONV_A_HALO, SC_CONV, ts)
    cv_buf[0:CONV_A_HALO, :] = cv_buf[ts:ts + CONV_A_HALO, :]
    act_b = (p_b[:, 2 * BRANCH_W:] * cb).astype(bf16)
    gate_d = gate(3)
    gate_a = gate(0)
    gate_b = gate(1)
    gate_c = gate(2)

    bits = pltpu.bitcast(gate_c[0:SUBLANES, 0:LANES], jnp.uint32)
    zero = pltpu.bitcast((bits >> 16) >> 16, f32)
    zero = jnp.concatenate([zero] * (BF16_ROWS // SUBLANES), axis=0)
    for blk in range(n_blk):
        for pair in range(N_Q_HEADS // 2):
            rows, lanes = slice(blk * BLOCK, blk * BLOCK + BF16_ROWS), slice(pair * LANES, (pair + 1) * LANES)
            q_s[rows, lanes] = (q_s[rows, lanes].astype(f32) + zero).astype(bf16)
    low_q = lax.broadcasted_iota(jnp.int32, (BLOCK, LANES), 1) < HEAD_DIM
    scores = {}
    for blk in range(n_blk):
        keys = slice(blk * BLOCK, (blk + 2) * BLOCK)
        for pair in range(N_Q_HEADS // 2):
            kvh = (2 * pair) // GQA_GROUP
            q2 = q_s[blk * BLOCK:(blk + 1) * BLOCK, pair * LANES:(pair + 1) * LANES]
            zq = jnp.zeros_like(q2)
            qq = jnp.concatenate([jnp.where(low_q, q2, zq), jnp.where(low_q, zq, q2)], axis=0)
            scores[blk, pair] = lax.dot_general(qq, kv_buf[kvh, keys, :], (((1,), (1,)), ((), ())),
                                                preferred_element_type=f32)

    _causal_conv_wide(d_buf, cdw_ref, cd_buf, CONV_D_HALO, CF_CONV, ts)
    d_buf[0:CONV_D_HALO, :] = d_buf[ts:ts + CONV_D_HALO, :]
    cd = cd_buf[...] + cdb
    mu = jnp.mean(cd, axis=-1, keepdims=True)
    var = jnp.mean(jnp.square(cd - mu), axis=-1, keepdims=True)
    ln = (cd - mu) * lax.rsqrt(var + EPS) * lng + lnb
    act_d = jax.nn.silu(ln).astype(bf16)
    merged = gate_d * out_proj(act_d, wdo_ref)

    assert WINDOW == BLOCK
    qi = lax.broadcasted_iota(jnp.int32, (BLOCK, BLOCK), 0)
    ki = lax.broadcasted_iota(jnp.int32, (BLOCK, BLOCK), 1)
    from_prev = ki > qi
    dist_f = jnp.where(from_prev, qi + BLOCK - ki, qi - ki).astype(f32)
    no_prev = from_prev & (j == 0)
    probs = {}
    for blk in range(n_blk):
        for pair in range(N_Q_HEADS // 2):
            both = []
            for half in range(2):
                head = 2 * pair + half
                s2 = scores[blk, pair][half * BLOCK:(half + 1) * BLOCK, :]
                s = jnp.where(from_prev, s2[:, :BLOCK], s2[:, BLOCK:]) - ALIBI_SLOPES[head] * dist_f
                if blk == 0:
                    s = jnp.where(no_prev, NEG_INF, s)
                sink = sinks_ref[head]
                m = jnp.maximum(jnp.max(s, axis=-1, keepdims=True), sink)
                e = jnp.exp(s - m)
                denom = jnp.sum(e, axis=-1, keepdims=True) + jnp.exp(sink - m)
                p = e / denom
                both += [jnp.where(from_prev, p, 0.0).astype(bf16), jnp.where(from_prev, 0.0, p).astype(bf16)]
            probs[blk, pair] = jnp.concatenate(both, axis=1)
    merged = merged + gate_a * out_proj(act_a, wao_ref)
    blocks_out = []
    for blk in range(n_blk):
        keys = slice(blk * BLOCK, (blk + 2) * BLOCK)
        pair_out = []
        for pair in range(N_Q_HEADS // 2):
            kvh = (2 * pair) // GQA_GROUP
            vv = jnp.concatenate([kv_buf[2 + 2 * kvh, keys, :], kv_buf[3 + 2 * kvh, keys, :]], axis=0)
            pair_out.append(jnp.dot(probs[blk, pair], vv, preferred_element_type=f32))
        blocks_out.append(jnp.concatenate(pair_out, axis=1))
    act_c = jnp.concatenate(blocks_out, axis=0).astype(bf16)
    kv_buf[:, 0:BLOCK, :] = kv_buf[:, ts:ts + BLOCK, :]

    merged = merged + gate_b * out_proj(act_b, wbo_ref)
    merged = merged + gate_c * out_proj(act_c, wco_ref)
    o_ref[0] = x + _bdot(merged, wo_ref[...])


def _ffn_kernel(x_ref, g_ref, wg_ref, wu_ref, wd_ref, fg_ref, o_ref, *, final_norm):
    x = x_ref[...]
    xn = _rms(x, g_ref[...]).astype(jnp.bfloat16)
    gte = jnp.dot(xn, wg_ref[...], preferred_element_type=jnp.float32)
    up = jnp.dot(xn, wu_ref[...], preferred_element_type=jnp.float32)
    y = x + _bdot(jax.nn.silu(gte) * up, wd_ref[...])
    if final_norm:
        y = _rms(y, fg_ref[...])
    o_ref[...] = y


def _with_casts(body, n_in, n_cast):
    def kern(*refs):
        ins, cast_in = refs[:n_in], refs[n_in:n_in + n_cast]
        out, cast_out = refs[n_in + n_cast], refs[n_in + n_cast + 1:n_in + 2 * n_cast + 1]
        body(*ins, out, *refs[n_in + 2 * n_cast + 1:])
        for src, dst in zip(cast_in, cast_out):
            dst[...] = src[...].astype(dst.dtype)
    return kern


def _layer_spec(arr, layer):
    rest = arr.shape[1:]
    return pl.BlockSpec((None,) + rest, lambda *_: (layer,) + (0,) * len(rest), pipeline_mode=pl.Buffered(1))


def _cast_plan(weights, layer, steps, step_of):
    views, in_specs, out_specs, out_shapes = [], [], [], []
    for w in weights:
        depth, r, c = w.shape
        rows = next(n for n in range(BF16_ROWS, r + 1, BF16_ROWS) if r % n == 0 and r // n <= steps)
        chunks = r // rows
        views.append(w.reshape(depth, chunks, rows, c))
        in_specs.append(pl.BlockSpec(
            (None, None, rows, c), lambda *g, n=chunks: (layer, jnp.minimum(step_of(*g), n - 1), 0, 0)))
        out_specs.append(pl.BlockSpec((None, rows, c), lambda *g, n=chunks: (jnp.minimum(step_of(*g), n - 1), 0, 0)))
        out_shapes.append(jax.ShapeDtypeStruct((chunks, rows, c), jnp.bfloat16))
    return views, in_specs, out_specs, out_shapes


def _mixer(x, params, cast_weights, cast_layer):
    b, s, d = x.shape
    n_j = s // TS
    x_spec = pl.BlockSpec((1, TS, d), lambda i, j: (i, j, 0))
    (sinks, sink_layer), params = params[0], params[1:]
    in_specs = [pl.BlockSpec(memory_space=pltpu.SMEM), x_spec] + [_layer_spec(p, l) for p, l in params]
    views, c_in, c_out, c_shapes = _cast_plan(cast_weights, cast_layer, b * n_j, lambda i, j: i * n_j + j)
    f32, bf16 = jnp.float32, jnp.bfloat16
    scratch = [
        pltpu.VMEM((TS, d), bf16),
        pltpu.VMEM((CONV_A_HALO + TS, BRANCH_W), f32),
        pltpu.VMEM((CONV_A_HALO + TS, BRANCH_W), f32),
        pltpu.VMEM((CONV_D_HALO + TS, BRANCH_W), f32),
        pltpu.VMEM((TS, BRANCH_W), f32),
        pltpu.VMEM((TS, BRANCH_W), bf16),
        pltpu.VMEM((6, BLOCK + TS, LANES), bf16),
        pltpu.VMEM((SUBLANES, BRANCH_W), f32),
    ]
    out, *cast = pl.pallas_call(
        _with_casts(_mixer_kernel, len(in_specs), len(views)),
        grid=(b, n_j),
        in_specs=in_specs + c_in,
        out_specs=[x_spec] + c_out,
        out_shape=[jax.ShapeDtypeStruct(x.shape, x.dtype)] + c_shapes,
        scratch_shapes=scratch,
        compiler_params=pltpu.CompilerParams(
            dimension_semantics=("arbitrary", "arbitrary"), vmem_limit_bytes=VMEM_LIMIT_MIXER),
        name="mixer",
    )(sinks[sink_layer], x, *[p for p, _ in params], *views)
    return out, [c.reshape(1, -1, c.shape[-1]) for c in cast]


def _ffn(x2, params, final_g, final_norm, cast_weights, cast_layer):
    t, d = x2.shape
    x_spec = pl.BlockSpec((TM, d), lambda i: (i, 0))
    fg_spec = pl.BlockSpec(final_g.shape, lambda i: (0, 0), pipeline_mode=pl.Buffered(1))
    in_specs = [x_spec] + [_layer_spec(p, l) for p, l in params] + [fg_spec]
    views, c_in, c_out, c_shapes = _cast_plan(cast_weights, cast_layer, t // TM, lambda i: i)
    out, *cast = pl.pallas_call(
        _with_casts(functools.partial(_ffn_kernel, final_norm=final_norm), len(in_specs), len(views)),
        grid=(t // TM,),
        in_specs=in_specs + c_in,
        out_specs=[x_spec] + c_out,
        out_shape=[jax.ShapeDtypeStruct(x2.shape, x2.dtype)] + c_shapes,
        compiler_params=pltpu.CompilerParams(
            dimension_semantics=("arbitrary",), vmem_limit_bytes=VMEM_LIMIT_FFN),
        name="ffn",
    )(x2, *[p for p, _ in params], final_g, *views)
    return out, [c.reshape(1, -1, c.shape[-1]) for c in cast]


def _block_diag(w):
    l, h, wb, _ = w.shape
    per = MXU_N // wb
    eye = jnp.eye(per, dtype=w.dtype)
    w = w.reshape(l, h // per, per, wb, wb)
    return (w[:, :, :, :, None, :] * eye[None, None, :, None, :, None]).reshape(l, h // per, MXU_N, MXU_N)


def kernel(x, norm1_g, w_in, conv_a_w, conv_a_b, lru_wx, lru_bx, lru_wa, lru_ba, lru_lambda, w_a_out, conv_b_w, w_b_out, sinks, w_c_out, conv_d_w, conv_d_b, ln_d_g, ln_d_b, w_d_out, w_o, norm2_g, w_ffn_gate, w_ffn_up, w_ffn_down, final_g):
    b, s, d = x.shape
    depth = w_in.shape[0]
    bf16 = jnp.bfloat16
    rows = lambda a: a.reshape(depth, 1, -1)
    wx_bd = _block_diag(lru_wx).astype(bf16)
    wa_bd = _block_diag(lru_wa).astype(bf16)
    mixer_w = (w_in, w_a_out, w_b_out, w_c_out, w_d_out, w_o)
    ffn_w = (w_ffn_gate, w_ffn_up, w_ffn_down)
    vecs = dict(norm1_g=norm1_g, conv_a_b=conv_a_b, lru_bx=lru_bx, lru_ba=lru_ba, lru_lambda=lru_lambda,
                conv_d_b=conv_d_b, ln_d_g=ln_d_g, ln_d_b=ln_d_b)
    vec = jnp.concatenate([vecs[name] for name in VEC_LAYOUT], axis=-1)[:, None, :]
    win, wao, wbo, wco, wdo, wo = [(w[0:1].astype(bf16), 0) for w in mixer_w]
    for l in range(depth):
        mixer_params = [
            (sinks, l), (vec, l), win, (conv_a_w, l), (wx_bd, l), (wa_bd, l), wao, (conv_b_w, l), wbo, wco,
            (conv_d_w, l), wdo, wo,
        ]
        x, ffn_bf = _mixer(x, mixer_params, ffn_w, l)
        ffn_params = [(rows(norm2_g), l)] + [(w, 0) for w in ffn_bf]
        last = l == depth - 1
        x, nxt = _ffn(x.reshape(b * s, d), ffn_params, final_g.reshape(1, -1), last, () if last else mixer_w, l + 1)
        x = x.reshape(b, s, d)
        if not last:
            win, wao, wbo, wco, wdo, wo = [(w, 0) for w in nxt]
    return x
```

```python
import functools

import jax
import jax.numpy as jnp
from jax import lax
from jax.experimental import pallas as pl
from jax.experimental.pallas import tpu as pltpu

D_MODEL = 1024
BRANCH_W = 512
LRU_CONV = 4
LRU_C = 8.0
SC_CONV = 3
HEAD_DIM = 64
N_Q_HEADS = 8
N_KV_HEADS = 2
GQA_GROUP = N_Q_HEADS // N_KV_HEADS
WINDOW = 128
BLOCK = 128
CF_CONV = 31
N_BRANCHES = 4
EPS = 1e-6
NEG_INF = -1e30

OFF_A = 0
OFF_B = OFF_A + 2 * BRANCH_W
OFF_C = OFF_B + 3 * BRANCH_W
OFF_D = OFF_C + BRANCH_W + 2 * N_KV_HEADS * HEAD_DIM
OFF_G = OFF_D + 2 * BRANCH_W

SUBLANES = 8
LANES = 128
BF16_ROWS = 16
MXU_N = 256
CONV_A_HALO = 8
CONV_D_HALO = 32
CONV_D_ROWS = 64
TS = 512
TM = 512
VMEM_LIMIT_MIXER = 56 * 1024 * 1024
VMEM_LIMIT_FFN = 56 * 1024 * 1024

VEC_LAYOUT = {}
for _name, _width in (("norm1_g", D_MODEL), ("conv_a_b", BRANCH_W), ("lru_bx", BRANCH_W), ("lru_ba", BRANCH_W),
                      ("lru_lambda", BRANCH_W), ("conv_d_b", BRANCH_W), ("ln_d_g", BRANCH_W), ("ln_d_b", BRANCH_W)):
    VEC_LAYOUT[_name] = (sum(w for _, w in VEC_LAYOUT.values()), _width)

ALIBI_SLOPES = tuple(2.0 ** (-8.0 * (i + 1) / N_Q_HEADS) for i in range(N_Q_HEADS))


def _rms(x, g):
    return x * lax.rsqrt(jnp.mean(x * x, axis=-1, keepdims=True) + EPS) * g


def _bdot(a, b):
    return jnp.dot(a.astype(jnp.bfloat16), b, preferred_element_type=jnp.float32)


def _causal_conv(buf_ref, w_ref, halo, width, rows):
    acc = None
    for k in range(width):
        term = w_ref[k:k + 1, :] * buf_ref[pl.ds(halo - (width - 1) + k, rows), :]
        acc = term if acc is None else acc + term
    return acc


def _causal_conv_wide(buf_ref, w_ref, out_ref, halo, width, rows):
    n_a = -(-width // SUBLANES)
    assert halo >= SUBLANES * n_a and rows % CONV_D_ROWS == 0
    ext = CONV_D_ROWS + SUBLANES
    for g in range(BRANCH_W // LANES):
        lanes = slice(g * LANES, (g + 1) * LANES)
        for t0 in range(0, rows, CONV_D_ROWS):
            slabs = [buf_ref[pl.ds(halo + t0 - SUBLANES * (a + 1), ext), lanes] for a in range(n_a)]
            y = None
            for r in range(SUBLANES):
                z = None
                for a in range(n_a):
                    k = width - 1 - (SUBLANES * a + r)
                    if k < 0:
                        continue
                    term = w_ref[k:k + 1, lanes] * slabs[a]
                    z = term if z is None else z + term
                shifted = z[SUBLANES - r:SUBLANES - r + CONV_D_ROWS, :]
                y = shifted if y is None else y + shifted
            out_ref[t0:t0 + CONV_D_ROWS, lanes] = y


def _mixer_kernel(sinks_ref, x_ref, vec_ref, w_in_ref, caw_ref, wx_ref, wa_ref, wao_ref, cbw_ref, wbo_ref, wco_ref,
                  cdw_ref, wdo_ref, wo_ref, o_ref, xn_s, ax_buf, cv_buf, d_buf, cd_buf, q_s, kv_buf, hc_ref):
    j = pl.program_id(1)
    ts = x_ref.shape[1]
    n_blk = ts // BLOCK
    bf16 = jnp.bfloat16
    f32 = jnp.float32
    cur_a = slice(CONV_A_HALO, CONV_A_HALO + ts)

    def vec(name):
        off, width = VEC_LAYOUT[name]
        return vec_ref[:, off:off + width]

    n1g, cab, bx, ba, lam = vec("norm1_g"), vec("conv_a_b"), vec("lru_bx"), vec("lru_ba"), vec("lru_lambda")
    cdb, lng, lnb = vec("conv_d_b"), vec("ln_d_g"), vec("ln_d_b")

    def proj(off, width):
        return jnp.dot(xn_s[...], w_in_ref[:, off:off + width], preferred_element_type=f32)

    def gate(idx):
        return jax.nn.sigmoid(proj(OFF_G + idx * D_MODEL, D_MODEL))

    def out_proj(act, w_ref):
        return jnp.dot(act, w_ref[...], preferred_element_type=f32)

    @pl.when(j == 0)
    def _():
        ax_buf[0:CONV_A_HALO, :] = jnp.zeros((CONV_A_HALO, BRANCH_W), f32)
        cv_buf[0:CONV_A_HALO, :] = jnp.zeros((CONV_A_HALO, BRANCH_W), f32)
        d_buf[0:CONV_D_HALO, :] = jnp.zeros((CONV_D_HALO, BRANCH_W), f32)
        kv_buf[:, 0:BLOCK, :] = jnp.zeros((6, BLOCK, LANES), bf16)
        hc_ref[...] = jnp.zeros_like(hc_ref)

    x = x_ref[0]
    xn_s[...] = _rms(x, n1g).astype(bf16)

    p_a = proj(OFF_A, 2 * BRANCH_W)
    ax_buf[cur_a, :] = p_a[:, :BRANCH_W]
    c = _causal_conv(ax_buf, caw_ref, CONV_A_HALO, LRU_CONV, ts) + cab
    ax_buf[0:CONV_A_HALO, :] = ax_buf[ts:ts + CONV_A_HALO, :]

    p_d = proj(OFF_D, 2 * BRANCH_W)
    d_buf[CONV_D_HALO:CONV_D_HALO + ts, :] = p_d[:, :BRANCH_W] * jax.nn.sigmoid(p_d[:, BRANCH_W:])

    c_bf = c.astype(bf16)

    def lru_gate(w_ref, bias):
        parts = [jnp.dot(c_bf[:, i * MXU_N:(i + 1) * MXU_N], w_ref[i], preferred_element_type=f32)
                 for i in range(BRANCH_W // MXU_N)]
        return jax.nn.sigmoid(jnp.concatenate(parts, axis=1) + bias)

    gate_i = lru_gate(wx_ref, bx)
    gate_r = lru_gate(wa_ref, ba)
    z = -lam
    softplus = jnp.maximum(z, 0.0) + jnp.log1p(jnp.exp(-jnp.abs(z)))
    log_a = -LRU_C * gate_r * softplus
    a = jnp.exp(log_a)
    u = (c * gate_i) * jnp.sqrt(-jnp.tanh(log_a) * (a * a + 1.0))
    n_grp = ts // SUBLANES
    a3 = a.reshape(n_grp, SUBLANES, BRANCH_W)
    u3 = u.reshape(n_grp, SUBLANES, BRANCH_W)
    sub = lax.broadcasted_iota(jnp.int32, (n_grp, SUBLANES, BRANCH_W), 1)
    for step in (1, 2, 4):
        keep = sub >= step
        u3 = a3 * jnp.where(keep, pltpu.roll(u3, step, axis=1), 0.0) + u3
        a3 = a3 * jnp.where(keep, pltpu.roll(a3, step, axis=1), 1.0)
    carry = hc_ref[0:1, :]
    h_rows = []
    for i in range(n_grp):
        h_i = u3[i] + a3[i] * carry
        h_rows.append(h_i)
        carry = h_i[SUBLANES - 1:SUBLANES, :]
    hc_ref[0:1, :] = carry
    act_a = (jnp.concatenate(h_rows, axis=0) * jax.nn.gelu(p_a[:, BRANCH_W:])).astype(bf16)

    p_c = proj(OFF_C, BRANCH_W + 2 * N_KV_HEADS * HEAD_DIM)
    q_s[...] = (p_c[:, :BRANCH_W] * (HEAD_DIM ** -0.5)).astype(bf16)
    low = lax.broadcasted_iota(jnp.int32, (ts, LANES), 1) < HEAD_DIM
    k = p_c[:, BRANCH_W:BRANCH_W + LANES]
    v = p_c[:, BRANCH_W + LANES:BRANCH_W + 2 * LANES]
    k_sw = pltpu.roll(k, HEAD_DIM, axis=1)
    v_sw = pltpu.roll(v, HEAD_DIM, axis=1)
    zero = jnp.zeros_like(v)
    cur = slice(BLOCK, BLOCK + ts)
    kv_buf[0, cur, :] = jnp.where(low, k, k_sw).astype(bf16)
    kv_buf[1, cur, :] = jnp.where(low, k_sw, k).astype(bf16)
    kv_buf[2, cur, :] = jnp.where(low, v, zero).astype(bf16)
    kv_buf[3, cur, :] = jnp.where(low, zero, v_sw).astype(bf16)
    kv_buf[4, cur, :] = jnp.where(low, v_sw, zero).astype(bf16)
    kv_buf[5, cur, :] = jnp.where(low, zero, v).astype(bf16)

    p_b = proj(OFF_B, 3 * BRANCH_W)
    cv_buf[cur_a, :] = p_b[:, BRANCH_W:2 * BRANCH_W] * p_b[:, :BRANCH_W]
    cb = _causal_conv(cv_buf, cbw_ref, CONV_A_HALO, SC_CONV, ts)
    cv_buf[0:CONV_A_HALO, :] = cv_buf[ts:ts + CONV_A_HALO, :]
    act_b = (p_b[:, 2 * BRANCH_W:] * cb).astype(bf16)
    gate_d = gate(3)
    gate_a = gate(0)
    gate_b = gate(1)
    gate_c = gate(2)

    low_q = lax.broadcasted_iota(jnp.int32, (BLOCK, LANES), 1) < HEAD_DIM
    scores = {}
    for blk in range(n_blk):
        keys = slice(blk * BLOCK, (blk + 2) * BLOCK)
        for pair in range(N_Q_HEADS // 2):
            kvh = (2 * pair) // GQA_GROUP
            q2 = q_s[blk * BLOCK:(blk + 1) * BLOCK, pair * LANES:(pair + 1) * LANES]
            zq = jnp.zeros_like(q2)
            qq = jnp.concatenate([jnp.where(low_q, q2, zq), jnp.where(low_q, zq, q2)], axis=0)
            scores[blk, pair] = lax.dot_general(qq, kv_buf[kvh, keys, :], (((1,), (1,)), ((), ())),
                                                preferred_element_type=f32)

    _causal_conv_wide(d_buf, cdw_ref, cd_buf, CONV_D_HALO, CF_CONV, ts)
    d_buf[0:CONV_D_HALO, :] = d_buf[ts:ts + CONV_D_HALO, :]
    cd = cd_buf[...] + cdb
    mu = jnp.mean(cd, axis=-1, keepdims=True)
    var = jnp.mean(jnp.square(cd - mu), axis=-1, keepdims=True)
    ln = (cd - mu) * lax.rsqrt(var + EPS) * lng + lnb
    act_d = jax.nn.silu(ln).astype(bf16)
    merged = gate_d * out_proj(act_d, wdo_ref)

    assert WINDOW == BLOCK
    qi = lax.broadcasted_iota(jnp.int32, (BLOCK, BLOCK), 0)
    ki = lax.broadcasted_iota(jnp.int32, (BLOCK, BLOCK), 1)
    from_prev = ki > qi
    dist_f = jnp.where(from_prev, qi + BLOCK - ki, qi - ki).astype(f32)
    no_prev = from_prev & (j == 0)
    probs = {}
    for blk in range(n_blk):
        for pair in range(N_Q_HEADS // 2):
            both = []
            for half in range(2):
                head = 2 * pair + half
                s2 = scores[blk, pair][half * BLOCK:(half + 1) * BLOCK, :]
                s = jnp.where(from_prev, s2[:, :BLOCK], s2[:, BLOCK:]) - ALIBI_SLOPES[head] * dist_f
                if blk == 0:
                    s = jnp.where(no_prev, NEG_INF, s)
                sink = sinks_ref[head]
                m = jnp.maximum(jnp.max(s, axis=-1, keepdims=True), sink)
                e = jnp.exp(s - m)
                denom = jnp.sum(e, axis=-1, keepdims=True) + jnp.exp(sink - m)
                p = e / denom
                both += [jnp.where(from_prev, p, 0.0).astype(bf16), jnp.where(from_prev, 0.0, p).astype(bf16)]
            probs[blk, pair] = jnp.concatenate(both, axis=1)
    merged = merged + gate_a * out_proj(act_a, wao_ref)
    blocks_out = []
    for blk in range(n_blk):
        keys = slice(blk * BLOCK, (blk + 2) * BLOCK)
        pair_out = []
        for pair in range(N_Q_HEADS // 2):
            kvh = (2 * pair) // GQA_GROUP
            vv = jnp.concatenate([kv_buf[2 + 2 * kvh, keys, :], kv_buf[3 + 2 * kvh, keys, :]], axis=0)
            pair_out.append(jnp.dot(probs[blk, pair], vv, preferred_element_type=f32))
        blocks_out.append(jnp.concatenate(pair_out, axis=1))
    act_c = jnp.concatenate(blocks_out, axis=0).astype(bf16)
    kv_buf[:, 0:BLOCK, :] = kv_buf[:, ts:ts + BLOCK, :]

    merged = merged + gate_b * out_proj(act_b, wbo_ref)
    merged = merged + gate_c * out_proj(act_c, wco_ref)
    o_ref[0] = x + _bdot(merged, wo_ref[...])


def _ffn_kernel(x_ref, g_ref, wg_ref, wu_ref, wd_ref, fg_ref, o_ref, *, final_norm):
    x = x_ref[...]
    xn = _rms(x, g_ref[...]).astype(jnp.bfloat16)
    gte = jnp.dot(xn, wg_ref[...], preferred_element_type=jnp.float32)
    up = jnp.dot(xn, wu_ref[...], preferred_element_type=jnp.float32)
    y = x + _bdot(jax.nn.silu(gte) * up, wd_ref[...])
    if final_norm:
        y = _rms(y, fg_ref[...])
    o_ref[...] = y


def _with_casts(body, n_in, n_cast):
    def kern(*refs):
        ins, cast_in = refs[:n_in], refs[n_in:n_in + n_cast]
        out, cast_out = refs[n_in + n_cast], refs[n_in + n_cast + 1:n_in + 2 * n_cast + 1]
        body(*ins, out, *refs[n_in + 2 * n_cast + 1:])
        for src, dst in zip(cast_in, cast_out):
            dst[...] = src[...].astype(dst.dtype)
    return kern


def _layer_spec(arr, layer):
    rest = arr.shape[1:]
    return pl.BlockSpec((None,) + rest, lambda *_: (layer,) + (0,) * len(rest), pipeline_mode=pl.Buffered(1))


def _cast_plan(weights, layer, steps, step_of):
    views, in_specs, out_specs, out_shapes = [], [], [], []
    for w in weights:
        depth, r, c = w.shape
        rows = next(n for n in range(BF16_ROWS, r + 1, BF16_ROWS) if r % n == 0 and r // n <= steps)
        chunks = r // rows
        views.append(w.reshape(depth, chunks, rows, c))
        in_specs.append(pl.BlockSpec(
            (None, None, rows, c), lambda *g, n=chunks: (layer, jnp.minimum(step_of(*g), n - 1), 0, 0)))
        out_specs.append(pl.BlockSpec((None, rows, c), lambda *g, n=chunks: (jnp.minimum(step_of(*g), n - 1), 0, 0)))
        out_shapes.append(jax.ShapeDtypeStruct((chunks, rows, c), jnp.bfloat16))
    return views, in_specs, out_specs, out_shapes


def _mixer(x, params, cast_weights, cast_layer):
    b, s, d = x.shape
    n_j = s // TS
    x_spec = pl.BlockSpec((1, TS, d), lambda i, j: (i, j, 0))
    (sinks, sink_layer), params = params[0], params[1:]
    in_specs = [pl.BlockSpec(memory_space=pltpu.SMEM), x_spec] + [_layer_spec(p, l) for p, l in params]
    views, c_in, c_out, c_shapes = _cast_plan(cast_weights, cast_layer, b * n_j, lambda i, j: i * n_j + j)
    f32, bf16 = jnp.float32, jnp.bfloat16
    scratch = [
        pltpu.VMEM((TS, d), bf16),
        pltpu.VMEM((CONV_A_HALO + TS, BRANCH_W), f32),
        pltpu.VMEM((CONV_A_HALO + TS, BRANCH_W), f32),
        pltpu.VMEM((CONV_D_HALO + TS, BRANCH_W), f32),
        pltpu.VMEM((TS, BRANCH_W), f32),
        pltpu.VMEM((TS, BRANCH_W), bf16),
        pltpu.VMEM((6, BLOCK + TS, LANES), bf16),
        pltpu.VMEM((SUBLANES, BRANCH_W), f32),
    ]
    out, *cast = pl.pallas_call(
        _with_casts(_mixer_kernel, len(in_specs), len(views)),
        grid=(b, n_j),
        in_specs=in_specs + c_in,
        out_specs=[x_spec] + c_out,
        out_shape=[jax.ShapeDtypeStruct(x.shape, x.dtype)] + c_shapes,
        scratch_shapes=scratch,
        compiler_params=pltpu.CompilerParams(
            dimension_semantics=("arbitrary", "arbitrary"), vmem_limit_bytes=VMEM_LIMIT_MIXER),
        name="mixer",
    )(sinks[sink_layer], x, *[p for p, _ in params], *views)
    return out, [c.reshape(1, -1, c.shape[-1]) for c in cast]


def _ffn(x2, params, final_g, final_norm, cast_weights, cast_layer):
    t, d = x2.shape
    x_spec = pl.BlockSpec((TM, d), lambda i: (i, 0))
    fg_spec = pl.BlockSpec(final_g.shape, lambda i: (0, 0), pipeline_mode=pl.Buffered(1))
    in_specs = [x_spec] + [_layer_spec(p, l) for p, l in params] + [fg_spec]
    views, c_in, c_out, c_shapes = _cast_plan(cast_weights, cast_layer, t // TM, lambda i: i)
    out, *cast = pl.pallas_call(
        _with_casts(functools.partial(_ffn_kernel, final_norm=final_norm), len(in_specs), len(views)),
        grid=(t // TM,),
        in_specs=in_specs + c_in,
        out_specs=[x_spec] + c_out,
        out_shape=[jax.ShapeDtypeStruct(x2.shape, x2.dtype)] + c_shapes,
        compiler_params=pltpu.CompilerParams(
            dimension_semantics=("arbitrary",), vmem_limit_bytes=VMEM_LIMIT_FFN),
        name="ffn",
    )(x2, *[p for p, _ in params], final_g, *views)
    return out, [c.reshape(1, -1, c.shape[-1]) for c in cast]


def _block_diag(w):
    l, h, wb, _ = w.shape
    per = MXU_N // wb
    eye = jnp.eye(per, dtype=w.dtype)
    w = w.reshape(l, h // per, per, wb, wb)
    return (w[:, :, :, :, None, :] * eye[None, None, :, None, :, None]).reshape(l, h // per, MXU_N, MXU_N)


def kernel(x, norm1_g, w_in, conv_a_w, conv_a_b, lru_wx, lru_bx, lru_wa, lru_ba, lru_lambda, w_a_out, conv_b_w, w_b_out, sinks, w_c_out, conv_d_w, conv_d_b, ln_d_g, ln_d_b, w_d_out, w_o, norm2_g, w_ffn_gate, w_ffn_up, w_ffn_down, final_g):
    b, s, d = x.shape
    depth = w_in.shape[0]
    bf16 = jnp.bfloat16
    rows = lambda a: a.reshape(depth, 1, -1)
    wx_bd = _block_diag(lru_wx).astype(bf16)
    wa_bd = _block_diag(lru_wa).astype(bf16)
    mixer_w = (w_in, w_a_out, w_b_out, w_c_out, w_d_out, w_o)
    ffn_w = (w_ffn_gate, w_ffn_up, w_ffn_down)
    vecs = dict(norm1_g=norm1_g, conv_a_b=conv_a_b, lru_bx=lru_bx, lru_ba=lru_ba, lru_lambda=lru_lambda,
                conv_d_b=conv_d_b, ln_d_g=ln_d_g, ln_d_b=ln_d_b)
    vec = jnp.concatenate([vecs[name] for name in VEC_LAYOUT], axis=-1)[:, None, :]
    win, wao, wbo, wco, wdo, wo = [(w[0:1].astype(bf16), 0) for w in mixer_w]
    for l in range(depth):
        mixer_params = [
            (sinks, l), (vec, l), win, (conv_a_w, l), (wx_bd, l), (wa_bd, l), wao, (conv_b_w, l), wbo, wco,
            (conv_d_w, l), wdo, wo,
        ]
        x, ffn_bf = _mixer(x, mixer_params, ffn_w, l)
        ffn_params = [(rows(norm2_g), l)] + [(w, 0) for w in ffn_bf]
        last = l == depth - 1
        x, nxt = _ffn(x.reshape(b * s, d), ffn_params, final_g.reshape(1, -1), last, () if last else mixer_w, l + 1)
        x = x.reshape(b, s, d)
        if not last:
            win, wao, wbo, wco, wdo, wo = [(w, 0) for w in nxt]
    return x
```

```python
import functools

import jax
import jax.numpy as jnp
from jax import lax
from jax.experimental import pallas as pl
from jax.experimental.pallas import tpu as pltpu

D_MODEL = 1024
BRANCH_W = 512
LRU_CONV = 4
LRU_C = 8.0
SC_CONV = 3
HEAD_DIM = 64
N_Q_HEADS = 8
N_KV_HEADS = 2
GQA_GROUP = N_Q_HEADS // N_KV_HEADS
WINDOW = 128
BLOCK = 128
CF_CONV = 31
N_BRANCHES = 4
EPS = 1e-6
NEG_INF = -1e30

OFF_A = 0
OFF_B = OFF_A + 2 * BRANCH_W
OFF_C = OFF_B + 3 * BRANCH_W
OFF_D = OFF_C + BRANCH_W + 2 * N_KV_HEADS * HEAD_DIM
OFF_G = OFF_D + 2 * BRANCH_W

SUBLANES = 8
LANES = 128
BF16_ROWS = 16
MXU_N = 256
CONV_A_HALO = 8
CONV_D_HALO = 32
CONV_D_ROWS = 64
TS = 512
TM = 1024
FF_CHUNK = 768
VMEM_LIMIT_MIXER = 56 * 1024 * 1024
VMEM_LIMIT_FFN = 56 * 1024 * 1024

VEC_LAYOUT = {}
for _name, _width in (("norm1_g", D_MODEL), ("conv_a_b", BRANCH_W), ("lru_bx", BRANCH_W), ("lru_ba", BRANCH_W),
                      ("lru_lambda", BRANCH_W), ("conv_d_b", BRANCH_W), ("ln_d_g", BRANCH_W), ("ln_d_b", BRANCH_W)):
    VEC_LAYOUT[_name] = (sum(w for _, w in VEC_LAYOUT.values()), _width)

ALIBI_SLOPES = tuple(2.0 ** (-8.0 * (i + 1) / N_Q_HEADS) for i in range(N_Q_HEADS))


def _rms(x, g):
    return x * lax.rsqrt(jnp.mean(x * x, axis=-1, keepdims=True) + EPS) * g


def _bdot(a, b):
    return jnp.dot(a.astype(jnp.bfloat16), b, preferred_element_type=jnp.float32)


def _causal_conv(buf_ref, w_ref, halo, width, rows):
    acc = None
    for k in range(width):
        term = w_ref[k:k + 1, :] * buf_ref[pl.ds(halo - (width - 1) + k, rows), :]
        acc = term if acc is None else acc + term
    return acc


def _causal_conv_wide(buf_ref, w_ref, out_ref, halo, width, rows):
    n_a = -(-width // SUBLANES)
    assert halo >= SUBLANES * n_a and rows % CONV_D_ROWS == 0
    ext = CONV_D_ROWS + SUBLANES
    for g in range(BRANCH_W // LANES):
        lanes = slice(g * LANES, (g + 1) * LANES)
        for t0 in range(0, rows, CONV_D_ROWS):
            slabs = [buf_ref[pl.ds(halo + t0 - SUBLANES * (a + 1), ext), lanes] for a in range(n_a)]
            y = None
            for r in range(SUBLANES):
                z = None
                for a in range(n_a):
                    k = width - 1 - (SUBLANES * a + r)
                    if k < 0:
                        continue
                    term = w_ref[k:k + 1, lanes] * slabs[a]
                    z = term if z is None else z + term
                shifted = z[SUBLANES - r:SUBLANES - r + CONV_D_ROWS, :]
                y = shifted if y is None else y + shifted
            out_ref[t0:t0 + CONV_D_ROWS, lanes] = y


def _mixer_kernel(sinks_ref, x_ref, vec_ref, w_in_ref, caw_ref, wx_ref, wa_ref, wao_ref, cbw_ref, wbo_ref, wco_ref,
                  cdw_ref, wdo_ref, wo_ref, o_ref, xn_s, ax_buf, cv_buf, d_buf, cd_buf, q_s, kv_buf, hc_ref):
    j = pl.program_id(1)
    ts = x_ref.shape[1]
    n_blk = ts // BLOCK
    bf16 = jnp.bfloat16
    f32 = jnp.float32
    cur_a = slice(CONV_A_HALO, CONV_A_HALO + ts)

    def vec(name):
        off, width = VEC_LAYOUT[name]
        return vec_ref[:, off:off + width]

    n1g, cab, bx, ba, lam = vec("norm1_g"), vec("conv_a_b"), vec("lru_bx"), vec("lru_ba"), vec("lru_lambda")
    cdb, lng, lnb = vec("conv_d_b"), vec("ln_d_g"), vec("ln_d_b")

    def proj(off, width):
        return jnp.dot(xn_s[...], w_in_ref[:, off:off + width], preferred_element_type=f32)

    def gate(idx):
        return jax.nn.sigmoid(proj(OFF_G + idx * D_MODEL, D_MODEL))

    def out_proj(act, w_ref):
        return jnp.dot(act, w_ref[...], preferred_element_type=f32)

    @pl.when(j == 0)
    def _():
        ax_buf[0:CONV_A_HALO, :] = jnp.zeros((CONV_A_HALO, BRANCH_W), f32)
        cv_buf[0:CONV_A_HALO, :] = jnp.zeros((CONV_A_HALO, BRANCH_W), f32)
        d_buf[0:CONV_D_HALO, :] = jnp.zeros((CONV_D_HALO, BRANCH_W), f32)
        kv_buf[:, 0:BLOCK, :] = jnp.zeros((6, BLOCK, LANES), bf16)
        hc_ref[...] = jnp.zeros_like(hc_ref)

    x = x_ref[0]
    xn_s[...] = _rms(x, n1g).astype(bf16)

    p_a = proj(OFF_A, 2 * BRANCH_W)
    ax_buf[cur_a, :] = p_a[:, :BRANCH_W]
    c = _causal_conv(ax_buf, caw_ref, CONV_A_HALO, LRU_CONV, ts) + cab
    ax_buf[0:CONV_A_HALO, :] = ax_buf[ts:ts + CONV_A_HALO, :]

    p_d = proj(OFF_D, 2 * BRANCH_W)
    d_buf[CONV_D_HALO:CONV_D_HALO + ts, :] = p_d[:, :BRANCH_W] * jax.nn.sigmoid(p_d[:, BRANCH_W:])

    c_bf = c.astype(bf16)

    def lru_gate(w_ref, bias):
        parts = [jnp.dot(c_bf[:, i * MXU_N:(i + 1) * MXU_N], w_ref[i], preferred_element_type=f32)
                 for i in range(BRANCH_W // MXU_N)]
        return jax.nn.sigmoid(jnp.concatenate(parts, axis=1) + bias)

    gate_i = lru_gate(wx_ref, bx)
    gate_r = lru_gate(wa_ref, ba)
    z = -lam
    softplus = jnp.maximum(z, 0.0) + jnp.log1p(jnp.exp(-jnp.abs(z)))
    log_a = -LRU_C * gate_r * softplus
    a = jnp.exp(log_a)
    u = (c * gate_i) * jnp.sqrt(-jnp.tanh(log_a) * (a * a + 1.0))
    n_grp = ts // SUBLANES
    a3 = a.reshape(n_grp, SUBLANES, BRANCH_W)
    u3 = u.reshape(n_grp, SUBLANES, BRANCH_W)
    sub = lax.broadcasted_iota(jnp.int32, (n_grp, SUBLANES, BRANCH_W), 1)
    for step in (1, 2, 4):
        keep = sub >= step
        u3 = a3 * jnp.where(keep, pltpu.roll(u3, step, axis=1), 0.0) + u3
        a3 = a3 * jnp.where(keep, pltpu.roll(a3, step, axis=1), 1.0)
    carry = hc_ref[0:1, :]
    h_rows = []
    for i in range(n_grp):
        h_i = u3[i] + a3[i] * carry
        h_rows.append(h_i)
        carry = h_i[SUBLANES - 1:SUBLANES, :]
    hc_ref[0:1, :] = carry
    act_a = (jnp.concatenate(h_rows, axis=0) * jax.nn.gelu(p_a[:, BRANCH_W:])).astype(bf16)

    p_c = proj(OFF_C, BRANCH_W + 2 * N_KV_HEADS * HEAD_DIM)
    q_s[...] = (p_c[:, :BRANCH_W] * (HEAD_DIM ** -0.5)).astype(bf16)
    low = lax.broadcasted_iota(jnp.int32, (ts, LANES), 1) < HEAD_DIM
    k = p_c[:, BRANCH_W:BRANCH_W + LANES]
    v = p_c[:, BRANCH_W + LANES:BRANCH_W + 2 * LANES]
    k_sw = pltpu.roll(k, HEAD_DIM, axis=1)
    v_sw = pltpu.roll(v, HEAD_DIM, axis=1)
    zero = jnp.zeros_like(v)
    cur = slice(BLOCK, BLOCK + ts)
    kv_buf[0, cur, :] = jnp.where(low, k, k_sw).astype(bf16)
    kv_buf[1, cur, :] = jnp.where(low, k_sw, k).astype(bf16)
    kv_buf[2, cur, :] = jnp.where(low, v, zero).astype(bf16)
    kv_buf[3, cur, :] = jnp.where(low, zero, v_sw).astype(bf16)
    kv_buf[4, cur, :] = jnp.where(low, v_sw, zero).astype(bf16)
    kv_buf[5, cur, :] = jnp.where(low, zero, v).astype(bf16)

    p_b = proj(OFF_B, 3 * BRANCH_W)
    cv_buf[cur_a, :] = p_b[:, BRANCH_W:2 * BRANCH_W] * p_b[:, :BRANCH_W]
    cb = _causal_conv(cv_buf, cbw_ref, CONV_A_HALO, SC_CONV, ts)
    cv_buf[0:CONV_A_HALO, :] = cv_buf[ts:ts + CONV_A_HALO, :]
    act_b = (p_b[:, 2 * BRANCH_W:] * cb).astype(bf16)
    gate_d = gate(3)
    gate_a = gate(0)
    gate_b = gate(1)
    gate_c = gate(2)

    low_q = lax.broadcasted_iota(jnp.int32, (BLOCK, LANES), 1) < HEAD_DIM
    scores = {}
    for blk in range(n_blk):
        keys = slice(blk * BLOCK, (blk + 2) * BLOCK)
        for pair in range(N_Q_HEADS // 2):
            kvh = (2 * pair) // GQA_GROUP
            q2 = q_s[blk * BLOCK:(blk + 1) * BLOCK, pair * LANES:(pair + 1) * LANES]
            zq = jnp.zeros_like(q2)
            qq = jnp.concatenate([jnp.where(low_q, q2, zq), jnp.where(low_q, zq, q2)], axis=0)
            scores[blk, pair] = lax.dot_general(qq, kv_buf[kvh, keys, :], (((1,), (1,)), ((), ())),
                                                preferred_element_type=f32)

    _causal_conv_wide(d_buf, cdw_ref, cd_buf, CONV_D_HALO, CF_CONV, ts)
    d_buf[0:CONV_D_HALO, :] = d_buf[ts:ts + CONV_D_HALO, :]
    cd = cd_buf[...] + cdb
    mu = jnp.mean(cd, axis=-1, keepdims=True)
    var = jnp.mean(jnp.square(cd - mu), axis=-1, keepdims=True)
    ln = (cd - mu) * lax.rsqrt(var + EPS) * lng + lnb
    act_d = jax.nn.silu(ln).astype(bf16)
    merged = gate_d * out_proj(act_d, wdo_ref)

    assert WINDOW == BLOCK
    qi = lax.broadcasted_iota(jnp.int32, (BLOCK, BLOCK), 0)
    ki = lax.broadcasted_iota(jnp.int32, (BLOCK, BLOCK), 1)
    from_prev = ki > qi
    dist_f = jnp.where(from_prev, qi + BLOCK - ki, qi - ki).astype(f32)
    no_prev = from_prev & (j == 0)
    probs = {}
    for blk in range(n_blk):
        for pair in range(N_Q_HEADS // 2):
            both = []
            for half in range(2):
                head = 2 * pair + half
                s2 = scores[blk, pair][half * BLOCK:(half + 1) * BLOCK, :]
                s = jnp.where(from_prev, s2[:, :BLOCK], s2[:, BLOCK:]) - ALIBI_SLOPES[head] * dist_f
                if blk == 0:
                    s = jnp.where(no_prev, NEG_INF, s)
                sink = sinks_ref[head]
                m = jnp.maximum(jnp.max(s, axis=-1, keepdims=True), sink)
                e = jnp.exp(s - m)
                denom = jnp.sum(e, axis=-1, keepdims=True) + jnp.exp(sink - m)
                p = e / denom
                both += [jnp.where(from_prev, p, 0.0).astype(bf16), jnp.where(from_prev, 0.0, p).astype(bf16)]
            probs[blk, pair] = jnp.concatenate(both, axis=1)
    merged = merged + gate_a * out_proj(act_a, wao_ref)
    blocks_out = []
    for blk in range(n_blk):
        keys = slice(blk * BLOCK, (blk + 2) * BLOCK)
        pair_out = []
        for pair in range(N_Q_HEADS // 2):
            kvh = (2 * pair) // GQA_GROUP
            vv = jnp.concatenate([kv_buf[2 + 2 * kvh, keys, :], kv_buf[3 + 2 * kvh, keys, :]], axis=0)
            pair_out.append(jnp.dot(probs[blk, pair], vv, preferred_element_type=f32))
        blocks_out.append(jnp.concatenate(pair_out, axis=1))
    act_c = jnp.concatenate(blocks_out, axis=0).astype(bf16)
    kv_buf[:, 0:BLOCK, :] = kv_buf[:, ts:ts + BLOCK, :]

    merged = merged + gate_b * out_proj(act_b, wbo_ref)
    merged = merged + gate_c * out_proj(act_c, wco_ref)
    o_ref[0] = x + _bdot(merged, wo_ref[...])


def _ffn_kernel(x_ref, g_ref, wg_ref, wu_ref, wd_ref, fg_ref, o_ref, *, final_norm):
    x = x_ref[...]
    xn = _rms(x, g_ref[...]).astype(jnp.bfloat16)
    y = x
    lo = 0
    while lo < wg_ref.shape[1]:
        hi = min(lo + FF_CHUNK, wg_ref.shape[1])
        gte = jnp.dot(xn, wg_ref[:, lo:hi], preferred_element_type=jnp.float32)
        up = jnp.dot(xn, wu_ref[:, lo:hi], preferred_element_type=jnp.float32)
        y = y + _bdot(jax.nn.silu(gte) * up, wd_ref[lo:hi, :])
        lo = hi
    if final_norm:
        y = _rms(y, fg_ref[...])
    o_ref[...] = y


def _with_casts(body, n_in, n_cast):
    def kern(*refs):
        ins, cast_in = refs[:n_in], refs[n_in:n_in + n_cast]
        out, cast_out = refs[n_in + n_cast], refs[n_in + n_cast + 1:n_in + 2 * n_cast + 1]
        body(*ins, out, *refs[n_in + 2 * n_cast + 1:])
        for src, dst in zip(cast_in, cast_out):
            dst[...] = src[...].astype(dst.dtype)
    return kern


def _layer_spec(arr, layer):
    rest = arr.shape[1:]
    return pl.BlockSpec((None,) + rest, lambda *_: (layer,) + (0,) * len(rest), pipeline_mode=pl.Buffered(1))


def _cast_plan(weights, layer, steps, step_of):
    views, in_specs, out_specs, out_shapes = [], [], [], []
    for w in weights:
        depth, r, c = w.shape
        rows = next(n for n in range(BF16_ROWS, r + 1, BF16_ROWS) if r % n == 0 and r // n <= steps)
        chunks = r // rows
        views.append(w.reshape(depth, chunks, rows, c))
        in_specs.append(pl.BlockSpec(
            (None, None, rows, c), lambda *g, n=chunks: (layer, jnp.minimum(step_of(*g), n - 1), 0, 0)))
        out_specs.append(pl.BlockSpec((None, rows, c), lambda *g, n=chunks: (jnp.minimum(step_of(*g), n - 1), 0, 0)))
        out_shapes.append(jax.ShapeDtypeStruct((chunks, rows, c), jnp.bfloat16))
    return views, in_specs, out_specs, out_shapes


def _mixer(x, params, cast_weights, cast_layer):
    b, s, d = x.shape
    n_j = s // TS
    x_spec = pl.BlockSpec((1, TS, d), lambda i, j: (i, j, 0))
    (sinks, sink_layer), params = params[0], params[1:]
    in_specs = [pl.BlockSpec(memory_space=pltpu.SMEM), x_spec] + [_layer_spec(p, l) for p, l in params]
    views, c_in, c_out, c_shapes = _cast_plan(cast_weights, cast_layer, b * n_j, lambda i, j: i * n_j + j)
    f32, bf16 = jnp.float32, jnp.bfloat16
    scratch = [
        pltpu.VMEM((TS, d), bf16),
        pltpu.VMEM((CONV_A_HALO + TS, BRANCH_W), f32),
        pltpu.VMEM((CONV_A_HALO + TS, BRANCH_W), f32),
        pltpu.VMEM((CONV_D_HALO + TS, BRANCH_W), f32),
        pltpu.VMEM((TS, BRANCH_W), f32),
        pltpu.VMEM((TS, BRANCH_W), bf16),
        pltpu.VMEM((6, BLOCK + TS, LANES), bf16),
        pltpu.VMEM((SUBLANES, BRANCH_W), f32),
    ]
    out, *cast = pl.pallas_call(
        _with_casts(_mixer_kernel, len(in_specs), len(views)),
        grid=(b, n_j),
        in_specs=in_specs + c_in,
        out_specs=[x_spec] + c_out,
        out_shape=[jax.ShapeDtypeStruct(x.shape, x.dtype)] + c_shapes,
        scratch_shapes=scratch,
        compiler_params=pltpu.CompilerParams(
            dimension_semantics=("arbitrary", "arbitrary"), vmem_limit_bytes=VMEM_LIMIT_MIXER),
        name="mixer",
    )(sinks[sink_layer], x, *[p for p, _ in params], *views)
    return out, [c.reshape(1, -1, c.shape[-1]) for c in cast]


def _ffn(x2, params, final_g, final_norm, cast_weights, cast_layer):
    t, d = x2.shape
    x_spec = pl.BlockSpec((TM, d), lambda i: (i, 0))
    fg_spec = pl.BlockSpec(final_g.shape, lambda i: (0, 0), pipeline_mode=pl.Buffered(1))
    in_specs = [x_spec] + [_layer_spec(p, l) for p, l in params] + [fg_spec]
    views, c_in, c_out, c_shapes = _cast_plan(cast_weights, cast_layer, t // TM, lambda i: i)
    out, *cast = pl.pallas_call(
        _with_casts(functools.partial(_ffn_kernel, final_norm=final_norm), len(in_specs), len(views)),
        grid=(t // TM,),
        in_specs=in_specs + c_in,
        out_specs=[x_spec] + c_out,
        out_shape=[jax.ShapeDtypeStruct(x2.shape, x2.dtype)] + c_shapes,
        compiler_params=pltpu.CompilerParams(
            dimension_semantics=("arbitrary",), vmem_limit_bytes=VMEM_LIMIT_FFN),
        name="ffn",
    )(x2, *[p for p, _ in params], final_g, *views)
    return out, [c.reshape(1, -1, c.shape[-1]) for c in cast]


def _block_diag(w):
    l, h, wb, _ = w.shape
    per = MXU_N // wb
    eye = jnp.eye(per, dtype=w.dtype)
    w = w.reshape(l, h // per, per, wb, wb)
    return (w[:, :, :, :, None, :] * eye[None, None, :, None, :, None]).reshape(l, h // per, MXU_N, MXU_N)


def kernel(x, norm1_g, w_in, conv_a_w, conv_a_b, lru_wx, lru_bx, lru_wa, lru_ba, lru_lambda, w_a_out, conv_b_w, w_b_out, sinks, w_c_out, conv_d_w, conv_d_b, ln_d_g, ln_d_b, w_d_out, w_o, norm2_g, w_ffn_gate, w_ffn_up, w_ffn_down, final_g):
    b, s, d = x.shape
    depth = w_in.shape[0]
    bf16 = jnp.bfloat16
    rows = lambda a: a.reshape(depth, 1, -1)
    wx_bd = _block_diag(lru_wx).astype(bf16)
    wa_bd = _block_diag(lru_wa).astype(bf16)
    mixer_w = (w_in, w_a_out, w_b_out, w_c_out, w_d_out, w_o)
    ffn_w = (w_ffn_gate, w_ffn_up, w_ffn_down)
    vecs = dict(norm1_g=norm1_g, conv_a_b=conv_a_b, lru_bx=lru_bx, lru_ba=lru_ba, lru_lambda=lru_lambda,
                conv_d_b=conv_d_b, ln_d_g=ln_d_g, ln_d_b=ln_d_b)
    vec = jnp.concatenate([vecs[name] for name in VEC_LAYOUT], axis=-1)[:, None, :]
    win, wao, wbo, wco, wdo, wo = [(w[0:1].astype(bf16), 0) for w in mixer_w]
    for l in range(depth):
        mixer_params = [
            (sinks, l), (vec, l), win, (conv_a_w, l), (wx_bd, l), (wa_bd, l), wao, (conv_b_w, l), wbo, wco,
            (conv_d_w, l), wdo, wo,
        ]
        x, ffn_bf = _mixer(x, mixer_params, ffn_w, l)
        ffn_params = [(rows(norm2_g), l)] + [(w, 0) for w in ffn_bf]
        last = l == depth - 1
        x, nxt = _ffn(x.reshape(b * s, d), ffn_params, final_g.reshape(1, -1), last, () if last else mixer_w, l + 1)
        x = x.reshape(b, s, d)
        if not last:
            win, wao, wbo, wco, wdo, wo = [(w, 0) for w in nxt]
    return x
```

```python
import functools

import jax
import jax.numpy as jnp
from jax import lax
from jax.experimental import pallas as pl
from jax.experimental.pallas import tpu as pltpu

D_MODEL = 1024
BRANCH_W = 512
LRU_CONV = 4
LRU_C = 8.0
SC_CONV = 3
HEAD_DIM = 64
N_Q_HEADS = 8
N_KV_HEADS = 2
GQA_GROUP = N_Q_HEADS // N_KV_HEADS
WINDOW = 128
BLOCK = 128
CF_CONV = 31
EPS = 1e-6
NEG_INF = -1e30

OFF_A = 0
OFF_B = OFF_A + 2 * BRANCH_W
OFF_C = OFF_B + 3 * BRANCH_W
OFF_D = OFF_C + BRANCH_W + 2 * N_KV_HEADS * HEAD_DIM
OFF_G = OFF_D + 2 * BRANCH_W

SUBLANES = 8
LANES = 128
BF16_ROWS = 16
MXU_N = 256
CONV_A_HALO = 8
CONV_D_HALO = 32
CONV_D_ROWS = 64
TS = 512
TM = 512
VMEM_BYTES = 64 * 1024 * 1024
VMEM_LIMIT = VMEM_BYTES - 8 * 1024 * 1024

VEC_LAYOUT = {}
for _name, _width in (("norm1_g", D_MODEL), ("conv_a_b", BRANCH_W), ("lru_bx", BRANCH_W), ("lru_ba", BRANCH_W),
                      ("lru_lambda", BRANCH_W), ("conv_d_b", BRANCH_W), ("ln_d_g", BRANCH_W), ("ln_d_b", BRANCH_W)):
    VEC_LAYOUT[_name] = (sum(w for _, w in VEC_LAYOUT.values()), _width)

ALIBI_SLOPES = tuple(2.0 ** (-8.0 * (i + 1) / N_Q_HEADS) for i in range(N_Q_HEADS))


def _rms(x, g):
    return x * lax.rsqrt(jnp.mean(x * x, axis=-1, keepdims=True) + EPS) * g


def _bdot(a, b):
    return jnp.dot(a.astype(jnp.bfloat16), b, preferred_element_type=jnp.float32)


def _causal_conv(buf_ref, w_ref, halo, width, rows):
    acc = None
    for k in range(width):
        term = w_ref[k:k + 1, :] * buf_ref[pl.ds(halo - (width - 1) + k, rows), :]
        acc = term if acc is None else acc + term
    return acc


def _causal_conv_wide(buf_ref, w_ref, out_ref, halo, width, rows):
    n_a = -(-width // SUBLANES)
    assert halo >= SUBLANES * n_a and rows % CONV_D_ROWS == 0
    ext = CONV_D_ROWS + SUBLANES
    for g in range(BRANCH_W // LANES):
        lanes = slice(g * LANES, (g + 1) * LANES)
        for t0 in range(0, rows, CONV_D_ROWS):
            slabs = [buf_ref[pl.ds(halo + t0 - SUBLANES * (a + 1), ext), lanes] for a in range(n_a)]
            y = None
            for r in range(SUBLANES):
                z = None
                for a in range(n_a):
                    k = width - 1 - (SUBLANES * a + r)
                    if k < 0:
                        continue
                    term = w_ref[k:k + 1, lanes] * slabs[a]
                    z = term if z is None else z + term
                shifted = z[SUBLANES - r:SUBLANES - r + CONV_D_ROWS, :]
                y = shifted if y is None else y + shifted
            out_ref[t0:t0 + CONV_D_ROWS, lanes] = y


def _mixer_kernel(sinks_ref, x_ref, vec_ref, w_in_ref, caw_ref, wx_ref, wa_ref, wao_ref, cbw_ref, wbo_ref, wco_ref,
                  cdw_ref, wdo_ref, wo_ref, o_ref, xn_s, ax_buf, cv_buf, d_buf, cd_buf, q_s, kv_buf, hc_ref):
    j = pl.program_id(1)
    ts = x_ref.shape[1]
    n_blk = ts // BLOCK
    bf16 = jnp.bfloat16
    f32 = jnp.float32
    cur_a = slice(CONV_A_HALO, CONV_A_HALO + ts)

    def vec(name):
        off, width = VEC_LAYOUT[name]
        return vec_ref[:, off:off + width]

    n1g, cab, bx, ba, lam = vec("norm1_g"), vec("conv_a_b"), vec("lru_bx"), vec("lru_ba"), vec("lru_lambda")
    cdb, lng, lnb = vec("conv_d_b"), vec("ln_d_g"), vec("ln_d_b")

    def proj(off, width):
        return jnp.dot(xn_s[...], w_in_ref[:, off:off + width], preferred_element_type=f32)

    def gate(idx):
        return jax.nn.sigmoid(proj(OFF_G + idx * D_MODEL, D_MODEL))

    def out_proj(act, w_ref):
        return jnp.dot(act, w_ref[...], preferred_element_type=f32)

    @pl.when(j == 0)
    def _():
        ax_buf[0:CONV_A_HALO, :] = jnp.zeros((CONV_A_HALO, BRANCH_W), f32)
        cv_buf[0:CONV_A_HALO, :] = jnp.zeros((CONV_A_HALO, BRANCH_W), f32)
        d_buf[0:CONV_D_HALO, :] = jnp.zeros((CONV_D_HALO, BRANCH_W), f32)
        kv_buf[:, 0:BLOCK, :] = jnp.zeros((6, BLOCK, LANES), bf16)
        hc_ref[...] = jnp.zeros_like(hc_ref)

    x = x_ref[0]
    xn_s[...] = _rms(x, n1g).astype(bf16)

    p_a = proj(OFF_A, 2 * BRANCH_W)
    ax_buf[cur_a, :] = p_a[:, :BRANCH_W]
    c = _causal_conv(ax_buf, caw_ref, CONV_A_HALO, LRU_CONV, ts) + cab
    ax_buf[0:CONV_A_HALO, :] = ax_buf[ts:ts + CONV_A_HALO, :]

    p_d = proj(OFF_D, 2 * BRANCH_W)
    d_buf[CONV_D_HALO:CONV_D_HALO + ts, :] = p_d[:, :BRANCH_W] * jax.nn.sigmoid(p_d[:, BRANCH_W:])

    c_bf = c.astype(bf16)

    def lru_gate(w_ref, bias):
        parts = [jnp.dot(c_bf[:, i * MXU_N:(i + 1) * MXU_N], w_ref[i], preferred_element_type=f32)
                 for i in range(BRANCH_W // MXU_N)]
        return jax.nn.sigmoid(jnp.concatenate(parts, axis=1) + bias)

    gate_i = lru_gate(wx_ref, bx)
    gate_r = lru_gate(wa_ref, ba)
    z = -lam
    softplus = jnp.maximum(z, 0.0) + jnp.log1p(jnp.exp(-jnp.abs(z)))
    log_a = -LRU_C * gate_r * softplus
    a = jnp.exp(log_a)
    u = (c * gate_i) * jnp.sqrt(-jnp.tanh(log_a) * (a * a + 1.0))
    n_grp = ts // SUBLANES
    a3 = a.reshape(n_grp, SUBLANES, BRANCH_W)
    u3 = u.reshape(n_grp, SUBLANES, BRANCH_W)
    sub = lax.broadcasted_iota(jnp.int32, (n_grp, SUBLANES, BRANCH_W), 1)
    for step in (1, 2, 4):
        keep = sub >= step
        u3 = a3 * jnp.where(keep, pltpu.roll(u3, step, axis=1), 0.0) + u3
        a3 = a3 * jnp.where(keep, pltpu.roll(a3, step, axis=1), 1.0)
    carry = hc_ref[0:1, :]
    h_rows = []
    for i in range(n_grp):
        h_i = u3[i] + a3[i] * carry
        h_rows.append(h_i)
        carry = h_i[SUBLANES - 1:SUBLANES, :]
    hc_ref[0:1, :] = carry
    act_a = (jnp.concatenate(h_rows, axis=0) * jax.nn.gelu(p_a[:, BRANCH_W:])).astype(bf16)

    p_c = proj(OFF_C, BRANCH_W + 2 * N_KV_HEADS * HEAD_DIM)
    q_s[...] = (p_c[:, :BRANCH_W] * (HEAD_DIM ** -0.5)).astype(bf16)
    low = lax.broadcasted_iota(jnp.int32, (ts, LANES), 1) < HEAD_DIM
    k = p_c[:, BRANCH_W:BRANCH_W + LANES]
    v = p_c[:, BRANCH_W + LANES:BRANCH_W + 2 * LANES]
    k_sw = pltpu.roll(k, HEAD_DIM, axis=1)
    v_sw = pltpu.roll(v, HEAD_DIM, axis=1)
    zero = jnp.zeros_like(v)
    cur = slice(BLOCK, BLOCK + ts)
    kv_buf[0, cur, :] = jnp.where(low, k, k_sw).astype(bf16)
    kv_buf[1, cur, :] = jnp.where(low, k_sw, k).astype(bf16)
    kv_buf[2, cur, :] = jnp.where(low, v, zero).astype(bf16)
    kv_buf[3, cur, :] = jnp.where(low, zero, v_sw).astype(bf16)
    kv_buf[4, cur, :] = jnp.where(low, v_sw, zero).astype(bf16)
    kv_buf[5, cur, :] = jnp.where(low, zero, v).astype(bf16)

    p_b = proj(OFF_B, 3 * BRANCH_W)
    cv_buf[cur_a, :] = p_b[:, BRANCH_W:2 * BRANCH_W] * p_b[:, :BRANCH_W]
    cb = _causal_conv(cv_buf, cbw_ref, CONV_A_HALO, SC_CONV, ts)
    cv_buf[0:CONV_A_HALO, :] = cv_buf[ts:ts + CONV_A_HALO, :]
    act_b = (p_b[:, 2 * BRANCH_W:] * cb).astype(bf16)
    gate_d = gate(3)
    gate_a = gate(0)
    gate_b = gate(1)
    gate_c = gate(2)

    low_q = lax.broadcasted_iota(jnp.int32, (BLOCK, LANES), 1) < HEAD_DIM
    scores = {}
    for blk in range(n_blk):
        keys = slice(blk * BLOCK, (blk + 2) * BLOCK)
        for pair in range(N_Q_HEADS // 2):
            kvh = (2 * pair) // GQA_GROUP
            q2 = q_s[blk * BLOCK:(blk + 1) * BLOCK, pair * LANES:(pair + 1) * LANES]
            zq = jnp.zeros_like(q2)
            qq = jnp.concatenate([jnp.where(low_q, q2, zq), jnp.where(low_q, zq, q2)], axis=0)
            scores[blk, pair] = lax.dot_general(qq, kv_buf[kvh, keys, :], (((1,), (1,)), ((), ())),
                                                preferred_element_type=f32)

    _causal_conv_wide(d_buf, cdw_ref, cd_buf, CONV_D_HALO, CF_CONV, ts)
    d_buf[0:CONV_D_HALO, :] = d_buf[ts:ts + CONV_D_HALO, :]
    cd = cd_buf[...] + cdb
    mu = jnp.mean(cd, axis=-1, keepdims=True)
    var = jnp.mean(jnp.square(cd - mu), axis=-1, keepdims=True)
    ln = (cd - mu) * lax.rsqrt(var + EPS) * lng + lnb
    act_d = jax.nn.silu(ln).astype(bf16)
    merged = gate_d * out_proj(act_d, wdo_ref)

    assert WINDOW == BLOCK
    qi = lax.broadcasted_iota(jnp.int32, (BLOCK, BLOCK), 0)
    ki = lax.broadcasted_iota(jnp.int32, (BLOCK, BLOCK), 1)
    from_prev = ki > qi
    dist_f = jnp.where(from_prev, qi + BLOCK - ki, qi - ki).astype(f32)
    no_prev = from_prev & (j == 0)
    probs = {}
    for blk in range(n_blk):
        for pair in range(N_Q_HEADS // 2):
            both = []
            for half in range(2):
                head = 2 * pair + half
                s2 = scores[blk, pair][half * BLOCK:(half + 1) * BLOCK, :]
                s = jnp.where(from_prev, s2[:, :BLOCK], s2[:, BLOCK:]) - ALIBI_SLOPES[head] * dist_f
                if blk == 0:
                    s = jnp.where(no_prev, NEG_INF, s)
                sink = sinks_ref[head]
                m = jnp.maximum(jnp.max(s, axis=-1, keepdims=True), sink)
                e = jnp.exp(s - m)
                denom = jnp.sum(e, axis=-1, keepdims=True) + jnp.exp(sink - m)
                p = e / denom
                both += [jnp.where(from_prev, p, 0.0).astype(bf16), jnp.where(from_prev, 0.0, p).astype(bf16)]
            probs[blk, pair] = jnp.concatenate(both, axis=1)
    merged = merged + gate_a * out_proj(act_a, wao_ref)
    blocks_out = []
    for blk in range(n_blk):
        keys = slice(blk * BLOCK, (blk + 2) * BLOCK)
        pair_out = []
        for pair in range(N_Q_HEADS // 2):
            kvh = (2 * pair) // GQA_GROUP
            vv = jnp.concatenate([kv_buf[2 + 2 * kvh, keys, :], kv_buf[3 + 2 * kvh, keys, :]], axis=0)
            pair_out.append(jnp.dot(probs[blk, pair], vv, preferred_element_type=f32))
        blocks_out.append(jnp.concatenate(pair_out, axis=1))
    act_c = jnp.concatenate(blocks_out, axis=0).astype(bf16)
    kv_buf[:, 0:BLOCK, :] = kv_buf[:, ts:ts + BLOCK, :]

    merged = merged + gate_b * out_proj(act_b, wbo_ref)
    merged = merged + gate_c * out_proj(act_c, wco_ref)
    o_ref[0] = x + _bdot(merged, wo_ref[...])


def _ffn_kernel(x_ref, g_ref, wg_ref, wu_ref, wd_ref, fg_ref, o_ref, *, final_norm):
    x = x_ref[...]
    xn = _rms(x, g_ref[...]).astype(jnp.bfloat16)
    gte = jnp.dot(xn, wg_ref[...], preferred_element_type=jnp.float32)
    up = jnp.dot(xn, wu_ref[...], preferred_element_type=jnp.float32)
    y = x + _bdot(jax.nn.silu(gte) * up, wd_ref[...])
    if final_norm:
        y = _rms(y, fg_ref[...])
    o_ref[...] = y


def _with_casts(body, n_in, n_cast):
    def kern(*refs):
        ins, cast_in = refs[:n_in], refs[n_in:n_in + n_cast]
        out, cast_out = refs[n_in + n_cast], refs[n_in + n_cast + 1:n_in + 2 * n_cast + 1]
        body(*ins, out, *refs[n_in + 2 * n_cast + 1:])
        for src, dst in zip(cast_in, cast_out):
            dst[...] = src[...].astype(dst.dtype)
    return kern


def _layer_spec(arr, layer):
    rest = arr.shape[1:]
    return pl.BlockSpec((None,) + rest, lambda *_: (layer,) + (0,) * len(rest), pipeline_mode=pl.Buffered(1))


def _cast_plan(weights, layer, steps, step_of):
    views, in_specs, out_specs, out_shapes = [], [], [], []
    for w in weights:
        depth, r, c = w.shape
        rows = next(n for n in range(BF16_ROWS, r + 1, BF16_ROWS) if r % n == 0 and r // n <= steps)
        chunks = r // rows
        views.append(w.reshape(depth, chunks, rows, c))
        in_specs.append(pl.BlockSpec(
            (None, None, rows, c), lambda *g, n=chunks: (layer, jnp.minimum(step_of(*g), n - 1), 0, 0)))
        out_specs.append(pl.BlockSpec((None, rows, c), lambda *g, n=chunks: (jnp.minimum(step_of(*g), n - 1), 0, 0)))
        out_shapes.append(jax.ShapeDtypeStruct((chunks, rows, c), jnp.bfloat16))
    return views, in_specs, out_specs, out_shapes


def _mixer(x, params, cast_weights, cast_layer):
    b, s, d = x.shape
    n_j = s // TS
    x_spec = pl.BlockSpec((1, TS, d), lambda i, j: (i, j, 0))
    (sinks, sink_layer), params = params[0], params[1:]
    in_specs = [pl.BlockSpec(memory_space=pltpu.SMEM), x_spec] + [_layer_spec(p, l) for p, l in params]
    views, c_in, c_out, c_shapes = _cast_plan(cast_weights, cast_layer, b * n_j, lambda i, j: i * n_j + j)
    f32, bf16 = jnp.float32, jnp.bfloat16
    scratch = [
        pltpu.VMEM((TS, d), bf16),
        pltpu.VMEM((CONV_A_HALO + TS, BRANCH_W), f32),
        pltpu.VMEM((CONV_A_HALO + TS, BRANCH_W), f32),
        pltpu.VMEM((CONV_D_HALO + TS, BRANCH_W), f32),
        pltpu.VMEM((TS, BRANCH_W), f32),
        pltpu.VMEM((TS, BRANCH_W), bf16),
        pltpu.VMEM((6, BLOCK + TS, LANES), bf16),
        pltpu.VMEM((SUBLANES, BRANCH_W), f32),
    ]
    out, *cast = pl.pallas_call(
        _with_casts(_mixer_kernel, len(in_specs), len(views)),
        grid=(b, n_j),
        in_specs=in_specs + c_in,
        out_specs=[x_spec] + c_out,
        out_shape=[jax.ShapeDtypeStruct(x.shape, x.dtype)] + c_shapes,
        scratch_shapes=scratch,
        compiler_params=pltpu.CompilerParams(
            dimension_semantics=("arbitrary", "arbitrary"), vmem_limit_bytes=VMEM_LIMIT),
        name="mixer",
    )(sinks[sink_layer], x, *[p for p, _ in params], *views)
    return out, [c.reshape(1, -1, c.shape[-1]) for c in cast]


def _ffn(x2, params, final_g, final_norm, cast_weights, cast_layer):
    t, d = x2.shape
    x_spec = pl.BlockSpec((TM, d), lambda i: (i, 0))
    fg_spec = pl.BlockSpec(final_g.shape, lambda i: (0, 0), pipeline_mode=pl.Buffered(1))
    in_specs = [x_spec] + [_layer_spec(p, l) for p, l in params] + [fg_spec]
    views, c_in, c_out, c_shapes = _cast_plan(cast_weights, cast_layer, t // TM, lambda i: i)
    out, *cast = pl.pallas_call(
        _with_casts(functools.partial(_ffn_kernel, final_norm=final_norm), len(in_specs), len(views)),
        grid=(t // TM,),
        in_specs=in_specs + c_in,
        out_specs=[x_spec] + c_out,
        out_shape=[jax.ShapeDtypeStruct(x2.shape, x2.dtype)] + c_shapes,
        compiler_params=pltpu.CompilerParams(
            dimension_semantics=("arbitrary",), vmem_limit_bytes=VMEM_LIMIT),
        name="ffn",
    )(x2, *[p for p, _ in params], final_g, *views)
    return out, [c.reshape(1, -1, c.shape[-1]) for c in cast]


def _block_diag(w):
    l, h, wb, _ = w.shape
    per = MXU_N // wb
    eye = jnp.eye(per, dtype=w.dtype)
    w = w.reshape(l, h // per, per, wb, wb)
    return (w[:, :, :, :, None, :] * eye[None, None, :, None, :, None]).reshape(l, h // per, MXU_N, MXU_N)


def kernel(x, norm1_g, w_in, conv_a_w, conv_a_b, lru_wx, lru_bx, lru_wa, lru_ba, lru_lambda, w_a_out, conv_b_w, w_b_out, sinks, w_c_out, conv_d_w, conv_d_b, ln_d_g, ln_d_b, w_d_out, w_o, norm2_g, w_ffn_gate, w_ffn_up, w_ffn_down, final_g):
    b, s, d = x.shape
    depth = w_in.shape[0]
    bf16 = jnp.bfloat16
    rows = lambda a: a.reshape(depth, 1, -1)
    wx_bd = _block_diag(lru_wx).astype(bf16)
    wa_bd = _block_diag(lru_wa).astype(bf16)
    mixer_w = (w_in, w_a_out, w_b_out, w_c_out, w_d_out, w_o)
    ffn_w = (w_ffn_gate, w_ffn_up, w_ffn_down)
    vecs = dict(norm1_g=norm1_g, conv_a_b=conv_a_b, lru_bx=lru_bx, lru_ba=lru_ba, lru_lambda=lru_lambda,
                conv_d_b=conv_d_b, ln_d_g=ln_d_g, ln_d_b=ln_d_b)
    vec = jnp.concatenate([vecs[name] for name in VEC_LAYOUT], axis=-1)[:, None, :]
    win, wao, wbo, wco, wdo, wo = [(w[0:1].astype(bf16), 0) for w in mixer_w]
    for l in range(depth):
        mixer_params = [
            (sinks, l), (vec, l), win, (conv_a_w, l), (wx_bd, l), (wa_bd, l), wao, (conv_b_w, l), wbo, wco,
            (conv_d_w, l), wdo, wo,
        ]
        x, ffn_bf = _mixer(x, mixer_params, ffn_w, l)
        ffn_params = [(rows(norm2_g), l)] + [(w, 0) for w in ffn_bf]
        last = l == depth - 1
        x, nxt = _ffn(x.reshape(b * s, d), ffn_params, final_g.reshape(1, -1), last, () if last else mixer_w, l + 1)
        x = x.reshape(b, s, d)
        if not last:
            win, wao, wbo, wco, wdo, wo = [(w, 0) for w in nxt]
    return x
```

```python
import functools

import jax
import jax.numpy as jnp
from jax import lax
from jax.experimental import pallas as pl
from jax.experimental.pallas import tpu as pltpu

D_MODEL = 1024
BRANCH_W = 512
LRU_CONV = 4
LRU_C = 8.0
SC_CONV = 3
HEAD_DIM = 64
N_Q_HEADS = 8
N_KV_HEADS = 2
GQA_GROUP = N_Q_HEADS // N_KV_HEADS
WINDOW = 128
BLOCK = 128
CF_CONV = 31
EPS = 1e-6
NEG_INF = -1e30

OFF_A = 0
OFF_B = OFF_A + 2 * BRANCH_W
OFF_C = OFF_B + 3 * BRANCH_W
OFF_D = OFF_C + BRANCH_W + 2 * N_KV_HEADS * HEAD_DIM
OFF_G = OFF_D + 2 * BRANCH_W

SUBLANES = 8
LANES = 128
BF16_ROWS = 16
MXU_N = 256
CONV_A_HALO = 8
CONV_D_HALO = 32
CONV_D_ROWS = 64
TS = 512
TM = 512
VMEM_BYTES = 64 * 1024 * 1024
VMEM_LIMIT = VMEM_BYTES - 8 * 1024 * 1024

VEC_LAYOUT = {}
for _name, _width in (("norm1_g", D_MODEL), ("conv_a_b", BRANCH_W), ("lru_bx", BRANCH_W), ("lru_ba", BRANCH_W),
                      ("lru_lambda", BRANCH_W), ("conv_d_b", BRANCH_W), ("ln_d_g", BRANCH_W), ("ln_d_b", BRANCH_W)):
    VEC_LAYOUT[_name] = (sum(w for _, w in VEC_LAYOUT.values()), _width)

ALIBI_SLOPES = tuple(2.0 ** (-8.0 * (i + 1) / N_Q_HEADS) for i in range(N_Q_HEADS))


def _rms(x, g):
    return x * lax.rsqrt(jnp.mean(x * x, axis=-1, keepdims=True) + EPS) * g


def _bdot(a, b):
    return jnp.dot(a.astype(jnp.bfloat16), b, preferred_element_type=jnp.float32)


def _causal_conv(buf_ref, w_ref, halo, width, rows):
    acc = None
    for k in range(width):
        term = w_ref[k:k + 1, :] * buf_ref[pl.ds(halo - (width - 1) + k, rows), :]
        acc = term if acc is None else acc + term
    return acc


def _causal_conv_wide(buf_ref, w_ref, out_ref, halo, width, rows):
    n_a = -(-width // SUBLANES)
    assert halo >= SUBLANES * n_a and rows % CONV_D_ROWS == 0
    ext = CONV_D_ROWS + SUBLANES
    for g in range(BRANCH_W // LANES):
        lanes = slice(g * LANES, (g + 1) * LANES)
        for t0 in range(0, rows, CONV_D_ROWS):
            slabs = [buf_ref[pl.ds(halo + t0 - SUBLANES * (a + 1), ext), lanes] for a in range(n_a)]
            y = None
            for r in range(SUBLANES):
                z = None
                for a in range(n_a):
                    k = width - 1 - (SUBLANES * a + r)
                    if k < 0:
                        continue
                    term = w_ref[k:k + 1, lanes] * slabs[a]
                    z = term if z is None else z + term
                shifted = z[SUBLANES - r:SUBLANES - r + CONV_D_ROWS, :]
                y = shifted if y is None else y + shifted
            out_ref[t0:t0 + CONV_D_ROWS, lanes] = y


def _mixer_kernel(sinks_ref, x_ref, vec_ref, w_in_ref, caw_ref, wx_ref, wa_ref, wao_ref, cbw_ref, wbo_ref, wco_ref,
                  cdw_ref, wdo_ref, wo_ref, o_ref, xn_s, ax_buf, cv_buf, d_buf, cd_buf, q_s, kv_buf, hc_ref):
    j = pl.program_id(1)
    ts = x_ref.shape[1]
    n_blk = ts // BLOCK
    bf16 = jnp.bfloat16
    f32 = jnp.float32
    cur_a = slice(CONV_A_HALO, CONV_A_HALO + ts)

    def vec(name):
        off, width = VEC_LAYOUT[name]
        return vec_ref[:, off:off + width]

    n1g, cab, bx, ba, lam = vec("norm1_g"), vec("conv_a_b"), vec("lru_bx"), vec("lru_ba"), vec("lru_lambda")
    cdb, lng, lnb = vec("conv_d_b"), vec("ln_d_g"), vec("ln_d_b")

    def proj(off, width):
        return jnp.dot(xn_s[...], w_in_ref[:, off:off + width], preferred_element_type=f32)

    def gate(idx):
        return jax.nn.sigmoid(proj(OFF_G + idx * D_MODEL, D_MODEL))

    def out_proj(act, w_ref):
        return jnp.dot(act, w_ref[...], preferred_element_type=f32)

    @pl.when(j == 0)
    def _():
        ax_buf[0:CONV_A_HALO, :] = jnp.zeros((CONV_A_HALO, BRANCH_W), f32)
        cv_buf[0:CONV_A_HALO, :] = jnp.zeros((CONV_A_HALO, BRANCH_W), f32)
        d_buf[0:CONV_D_HALO, :] = jnp.zeros((CONV_D_HALO, BRANCH_W), f32)
        kv_buf[:, 0:BLOCK, :] = jnp.zeros((6, BLOCK, LANES), bf16)
        hc_ref[...] = jnp.zeros_like(hc_ref)

    x = x_ref[0]
    xn_s[...] = _rms(x, n1g).astype(bf16)

    p_a = proj(OFF_A, 2 * BRANCH_W)
    ax_buf[cur_a, :] = p_a[:, :BRANCH_W]
    c = _causal_conv(ax_buf, caw_ref, CONV_A_HALO, LRU_CONV, ts) + cab
    ax_buf[0:CONV_A_HALO, :] = ax_buf[ts:ts + CONV_A_HALO, :]

    p_d = proj(OFF_D, 2 * BRANCH_W)
    d_buf[CONV_D_HALO:CONV_D_HALO + ts, :] = p_d[:, :BRANCH_W] * jax.nn.sigmoid(p_d[:, BRANCH_W:])

    z = -lam
    softplus = jnp.maximum(z, 0.0) + jnp.log1p(jnp.exp(-jnp.abs(z)))
    n_grp = ts // SUBLANES
    sub = lax.broadcasted_iota(jnp.int32, (n_grp, SUBLANES, MXU_N), 1)
    act_halves = []
    for half in range(BRANCH_W // MXU_N):
        cols = slice(half * MXU_N, (half + 1) * MXU_N)
        c_h = c[:, cols]
        c_bf = c_h.astype(bf16)
        gate_i = jax.nn.sigmoid(jnp.dot(c_bf, wx_ref[half], preferred_element_type=f32) + bx[:, cols])
        gate_r = jax.nn.sigmoid(jnp.dot(c_bf, wa_ref[half], preferred_element_type=f32) + ba[:, cols])
        log_a = -LRU_C * gate_r * softplus[:, cols]
        a = jnp.exp(log_a)
        u = (c_h * gate_i) * jnp.sqrt(-jnp.tanh(log_a) * (a * a + 1.0))
        a3 = a.reshape(n_grp, SUBLANES, MXU_N)
        u3 = u.reshape(n_grp, SUBLANES, MXU_N)
        for step in (1, 2, 4):
            keep = sub >= step
            u3 = a3 * jnp.where(keep, pltpu.roll(u3, step, axis=1), 0.0) + u3
            a3 = a3 * jnp.where(keep, pltpu.roll(a3, step, axis=1), 1.0)
        carry = hc_ref[0:1, cols]
        h_rows = []
        for i in range(n_grp):
            h_i = u3[i] + a3[i] * carry
            h_rows.append(h_i)
            carry = h_i[SUBLANES - 1:SUBLANES, :]
        hc_ref[0:1, cols] = carry
        gelu_gate = jax.nn.gelu(p_a[:, BRANCH_W + half * MXU_N:BRANCH_W + (half + 1) * MXU_N])
        act_halves.append((jnp.concatenate(h_rows, axis=0) * gelu_gate).astype(bf16))
    act_a = jnp.concatenate(act_halves, axis=1)

    p_c = proj(OFF_C, BRANCH_W + 2 * N_KV_HEADS * HEAD_DIM)
    q_s[...] = (p_c[:, :BRANCH_W] * (HEAD_DIM ** -0.5)).astype(bf16)
    low = lax.broadcasted_iota(jnp.int32, (ts, LANES), 1) < HEAD_DIM
    k = p_c[:, BRANCH_W:BRANCH_W + LANES]
    v = p_c[:, BRANCH_W + LANES:BRANCH_W + 2 * LANES]
    k_sw = pltpu.roll(k, HEAD_DIM, axis=1)
    v_sw = pltpu.roll(v, HEAD_DIM, axis=1)
    zero = jnp.zeros_like(v)
    cur = slice(BLOCK, BLOCK + ts)
    kv_buf[0, cur, :] = jnp.where(low, k, k_sw).astype(bf16)
    kv_buf[1, cur, :] = jnp.where(low, k_sw, k).astype(bf16)
    kv_buf[2, cur, :] = jnp.where(low, v, zero).astype(bf16)
    kv_buf[3, cur, :] = jnp.where(low, zero, v_sw).astype(bf16)
    kv_buf[4, cur, :] = jnp.where(low, v_sw, zero).astype(bf16)
    kv_buf[5, cur, :] = jnp.where(low, zero, v).astype(bf16)

    p_b = proj(OFF_B, 3 * BRANCH_W)
    cv_buf[cur_a, :] = p_b[:, BRANCH_W:2 * BRANCH_W] * p_b[:, :BRANCH_W]
    cb = _causal_conv(cv_buf, cbw_ref, CONV_A_HALO, SC_CONV, ts)
    cv_buf[0:CONV_A_HALO, :] = cv_buf[ts:ts + CONV_A_HALO, :]
    act_b = (p_b[:, 2 * BRANCH_W:] * cb).astype(bf16)
    gate_d = gate(3)
    gate_a = gate(0)
    gate_b = gate(1)
    gate_c = gate(2)

    low_q = lax.broadcasted_iota(jnp.int32, (BLOCK, LANES), 1) < HEAD_DIM
    scores = {}
    for blk in range(n_blk):
        keys = slice(blk * BLOCK, (blk + 2) * BLOCK)
        for pair in range(N_Q_HEADS // 2):
            kvh = (2 * pair) // GQA_GROUP
            q2 = q_s[blk * BLOCK:(blk + 1) * BLOCK, pair * LANES:(pair + 1) * LANES]
            zq = jnp.zeros_like(q2)
            qq = jnp.concatenate([jnp.where(low_q, q2, zq), jnp.where(low_q, zq, q2)], axis=0)
            scores[blk, pair] = lax.dot_general(qq, kv_buf[kvh, keys, :], (((1,), (1,)), ((), ())),
                                                preferred_element_type=f32)

    _causal_conv_wide(d_buf, cdw_ref, cd_buf, CONV_D_HALO, CF_CONV, ts)
    d_buf[0:CONV_D_HALO, :] = d_buf[ts:ts + CONV_D_HALO, :]
    cd = cd_buf[...] + cdb
    mu = jnp.mean(cd, axis=-1, keepdims=True)
    var = jnp.mean(jnp.square(cd - mu), axis=-1, keepdims=True)
    ln = (cd - mu) * lax.rsqrt(var + EPS) * lng + lnb
    act_d = jax.nn.silu(ln).astype(bf16)
    merged = gate_d * out_proj(act_d, wdo_ref)

    assert WINDOW == BLOCK
    qi = lax.broadcasted_iota(jnp.int32, (BLOCK, BLOCK), 0)
    ki = lax.broadcasted_iota(jnp.int32, (BLOCK, BLOCK), 1)
    from_prev = ki > qi
    dist_f = jnp.where(from_prev, qi + BLOCK - ki, qi - ki).astype(f32)
    no_prev = from_prev & (j == 0)
    probs = {}
    for blk in range(n_blk):
        for pair in range(N_Q_HEADS // 2):
            both = []
            for half in range(2):
                head = 2 * pair + half
                s2 = scores[blk, pair][half * BLOCK:(half + 1) * BLOCK, :]
                s = jnp.where(from_prev, s2[:, :BLOCK], s2[:, BLOCK:]) - ALIBI_SLOPES[head] * dist_f
                if blk == 0:
                    s = jnp.where(no_prev, NEG_INF, s)
                sink = sinks_ref[head]
                m = jnp.maximum(jnp.max(s, axis=-1, keepdims=True), sink)
                e = jnp.exp(s - m)
                denom = jnp.sum(e, axis=-1, keepdims=True) + jnp.exp(sink - m)
                p = e / denom
                both += [jnp.where(from_prev, p, 0.0).astype(bf16), jnp.where(from_prev, 0.0, p).astype(bf16)]
            probs[blk, pair] = jnp.concatenate(both, axis=1)
    merged = merged + gate_a * out_proj(act_a, wao_ref)
    blocks_out = []
    for blk in range(n_blk):
        keys = slice(blk * BLOCK, (blk + 2) * BLOCK)
        pair_out = []
        for pair in range(N_Q_HEADS // 2):
            kvh = (2 * pair) // GQA_GROUP
            vv = jnp.concatenate([kv_buf[2 + 2 * kvh, keys, :], kv_buf[3 + 2 * kvh, keys, :]], axis=0)
            pair_out.append(jnp.dot(probs[blk, pair], vv, preferred_element_type=f32))
        blocks_out.append(jnp.concatenate(pair_out, axis=1))
    act_c = jnp.concatenate(blocks_out, axis=0).astype(bf16)
    kv_buf[:, 0:BLOCK, :] = kv_buf[:, ts:ts + BLOCK, :]

    merged = merged + gate_b * out_proj(act_b, wbo_ref)
    merged = merged + gate_c * out_proj(act_c, wco_ref)
    o_ref[0] = x + _bdot(merged, wo_ref[...])


def _ffn_kernel(x_ref, g_ref, wg_ref, wu_ref, wd_ref, fg_ref, o_ref, *, final_norm):
    x = x_ref[...]
    xn = _rms(x, g_ref[...]).astype(jnp.bfloat16)
    gte = jnp.dot(xn, wg_ref[...], preferred_element_type=jnp.float32)
    up = jnp.dot(xn, wu_ref[...], preferred_element_type=jnp.float32)
    y = x + _bdot(jax.nn.silu(gte) * up, wd_ref[...])
    if final_norm:
        y = _rms(y, fg_ref[...])
    o_ref[...] = y


def _with_casts(body, n_in, n_cast):
    def kern(*refs):
        ins, cast_in = refs[:n_in], refs[n_in:n_in + n_cast]
        out, cast_out = refs[n_in + n_cast], refs[n_in + n_cast + 1:n_in + 2 * n_cast + 1]
        body(*ins, out, *refs[n_in + 2 * n_cast + 1:])
        for src, dst in zip(cast_in, cast_out):
            dst[...] = src[...].astype(dst.dtype)
    return kern


def _layer_spec(arr, layer):
    rest = arr.shape[1:]
    return pl.BlockSpec((None,) + rest, lambda *_: (layer,) + (0,) * len(rest), pipeline_mode=pl.Buffered(1))


def _cast_plan(weights, layer, steps, step_of):
    views, in_specs, out_specs, out_shapes = [], [], [], []
    for w in weights:
        depth, r, c = w.shape
        rows = next(n for n in range(BF16_ROWS, r + 1, BF16_ROWS) if r % n == 0 and r // n <= steps)
        chunks = r // rows
        views.append(w.reshape(depth, chunks, rows, c))
        in_specs.append(pl.BlockSpec(
            (None, None, rows, c), lambda *g, n=chunks: (layer, jnp.minimum(step_of(*g), n - 1), 0, 0)))
        out_specs.append(pl.BlockSpec((None, rows, c), lambda *g, n=chunks: (jnp.minimum(step_of(*g), n - 1), 0, 0)))
        out_shapes.append(jax.ShapeDtypeStruct((chunks, rows, c), jnp.bfloat16))
    return views, in_specs, out_specs, out_shapes


def _mixer(x, params, cast_weights, cast_layer):
    b, s, d = x.shape
    n_j = s // TS
    x_spec = pl.BlockSpec((1, TS, d), lambda i, j: (i, j, 0))
    (sinks, sink_layer), params = params[0], params[1:]
    in_specs = [pl.BlockSpec(memory_space=pltpu.SMEM), x_spec] + [_layer_spec(p, l) for p, l in params]
    views, c_in, c_out, c_shapes = _cast_plan(cast_weights, cast_layer, b * n_j, lambda i, j: i * n_j + j)
    f32, bf16 = jnp.float32, jnp.bfloat16
    scratch = [
        pltpu.VMEM((TS, d), bf16),
        pltpu.VMEM((CONV_A_HALO + TS, BRANCH_W), f32),
        pltpu.VMEM((CONV_A_HALO + TS, BRANCH_W), f32),
        pltpu.VMEM((CONV_D_HALO + TS, BRANCH_W), f32),
        pltpu.VMEM((TS, BRANCH_W), f32),
        pltpu.VMEM((TS, BRANCH_W), bf16),
        pltpu.VMEM((6, BLOCK + TS, LANES), bf16),
        pltpu.VMEM((SUBLANES, BRANCH_W), f32),
    ]
    out, *cast = pl.pallas_call(
        _with_casts(_mixer_kernel, len(in_specs), len(views)),
        grid=(b, n_j),
        in_specs=in_specs + c_in,
        out_specs=[x_spec] + c_out,
        out_shape=[jax.ShapeDtypeStruct(x.shape, x.dtype)] + c_shapes,
        scratch_shapes=scratch,
        compiler_params=pltpu.CompilerParams(
            dimension_semantics=("arbitrary", "arbitrary"), vmem_limit_bytes=VMEM_LIMIT),
        name="mixer",
    )(sinks[sink_layer], x, *[p for p, _ in params], *views)
    return out, [c.reshape(1, -1, c.shape[-1]) for c in cast]


def _ffn(x2, params, final_g, final_norm, cast_weights, cast_layer):
    t, d = x2.shape
    x_spec = pl.BlockSpec((TM, d), lambda i: (i, 0))
    fg_spec = pl.BlockSpec(final_g.shape, lambda i: (0, 0), pipeline_mode=pl.Buffered(1))
    in_specs = [x_spec] + [_layer_spec(p, l) for p, l in params] + [fg_spec]
    views, c_in, c_out, c_shapes = _cast_plan(cast_weights, cast_layer, t // TM, lambda i: i)
    out, *cast = pl.pallas_call(
        _with_casts(functools.partial(_ffn_kernel, final_norm=final_norm), len(in_specs), len(views)),
        grid=(t // TM,),
        in_specs=in_specs + c_in,
        out_specs=[x_spec] + c_out,
        out_shape=[jax.ShapeDtypeStruct(x2.shape, x2.dtype)] + c_shapes,
        compiler_params=pltpu.CompilerParams(
            dimension_semantics=("arbitrary",), vmem_limit_bytes=VMEM_LIMIT),
        name="ffn",
    )(x2, *[p for p, _ in params], final_g, *views)
    return out, [c.reshape(1, -1, c.shape[-1]) for c in cast]


def _block_diag(w):
    l, h, wb, _ = w.shape
    per = MXU_N // wb
    eye = jnp.eye(per, dtype=w.dtype)
    w = w.reshape(l, h // per, per, wb, wb)
    return (w[:, :, :, :, None, :] * eye[None, None, :, None, :, None]).reshape(l, h // per, MXU_N, MXU_N)


def kernel(x, norm1_g, w_in, conv_a_w, conv_a_b, lru_wx, lru_bx, lru_wa, lru_ba, lru_lambda, w_a_out, conv_b_w, w_b_out, sinks, w_c_out, conv_d_w, conv_d_b, ln_d_g, ln_d_b, w_d_out, w_o, norm2_g, w_ffn_gate, w_ffn_up, w_ffn_down, final_g):
    b, s, d = x.shape
    depth = w_in.shape[0]
    bf16 = jnp.bfloat16
    rows = lambda a: a.reshape(depth, 1, -1)
    wx_bd = _block_diag(lru_wx).astype(bf16)
    wa_bd = _block_diag(lru_wa).astype(bf16)
    mixer_w = (w_in, w_a_out, w_b_out, w_c_out, w_d_out, w_o)
    ffn_w = (w_ffn_gate, w_ffn_up, w_ffn_down)
    vecs = dict(norm1_g=norm1_g, conv_a_b=conv_a_b, lru_bx=lru_bx, lru_ba=lru_ba, lru_lambda=lru_lambda,
                conv_d_b=conv_d_b, ln_d_g=ln_d_g, ln_d_b=ln_d_b)
    vec = jnp.concatenate([vecs[name] for name in VEC_LAYOUT], axis=-1)[:, None, :]
    win, wao, wbo, wco, wdo, wo = [(w[0:1].astype(bf16), 0) for w in mixer_w]
    for l in range(depth):
        mixer_params = [
            (sinks, l), (vec, l), win, (conv_a_w, l), (wx_bd, l), (wa_bd, l), wao, (conv_b_w, l), wbo, wco,
            (conv_d_w, l), wdo, wo,
        ]
        x, ffn_bf = _mixer(x, mixer_params, ffn_w, l)
        ffn_params = [(rows(norm2_g), l)] + [(w, 0) for w in ffn_bf]
        last = l == depth - 1
        x, nxt = _ffn(x.reshape(b * s, d), ffn_params, final_g.reshape(1, -1), last, () if last else mixer_w, l + 1)
        x = x.reshape(b, s, d)
        if not last:
            win, wao, wbo, wco, wdo, wo = [(w, 0) for w in nxt]
    return x
```
